```python
import math
import jax, jax.numpy as jnp
from jax import lax
import numpy as np

D_MODEL = 1024
BATCH = 32
SEQ = 2048
DEPTH = 4

DA_HEADS = 4
DA_HEAD_DIM = 64
DA_V_DIM = 2 * DA_HEAD_DIM
DA_QK_WIDTH = DA_HEADS * 2 * DA_HEAD_DIM
DA_WIDTH = DA_HEADS * DA_V_DIM
Q_BLOCK = 128
HB_HEADS = 4
HB_HEAD_DIM = 64
HB_WIDTH = HB_HEADS * HB_HEAD_DIM
MAX_NEG_LOGIT = 80.0
GC_HEADS = 4
GC_KEY_DIM = 32
GC_VAL_DIM = 64
GC_KEY_WIDTH = GC_HEADS * GC_KEY_DIM
GC_WIDTH = GC_HEADS * GC_VAL_DIM
GC_GATE_RANK = 16
GC_GATE_NORMALIZER = 16.0
CHUNK = 64
N_BRANCH = 3
D_FF = 4 * D_MODEL
N_MOD = 6
EPS = 1e-6

IN_SIZES = (DA_QK_WIDTH, DA_QK_WIDTH, DA_WIDTH,
            HB_WIDTH, HB_WIDTH, HB_WIDTH, HB_WIDTH, HB_WIDTH,
            GC_KEY_WIDTH, GC_KEY_WIDTH, GC_WIDTH, GC_WIDTH,
            GC_GATE_RANK, GC_GATE_RANK,
            N_BRANCH * D_MODEL)
D_IN = (2 * DA_QK_WIDTH + DA_WIDTH + 5 * HB_WIDTH + 2 * GC_KEY_WIDTH + 2 * GC_WIDTH
        + 2 * GC_GATE_RANK + N_BRANCH * D_MODEL)

kernel_name = "hybrid_diffattn_hgrn2_gla_encoder"


def rms_norm(x, g):
    xf = x.astype(jnp.float32)
    y = xf * lax.rsqrt(jnp.mean(xf * xf, axis=-1, keepdims=True) + EPS)
    return (y * g.astype(jnp.float32)).astype(x.dtype)


def split_cols(t, sizes):
    offs = np.cumsum(sizes)[:-1].tolist()
    return jnp.split(t, offs, axis=-1)


def to_heads(t, n_heads):
    b, s, w = t.shape
    return t.reshape(b, s, n_heads, w // n_heads).transpose(0, 2, 1, 3)


def from_heads(t):
    b, n, s, h = t.shape
    return t.transpose(0, 2, 1, 3).reshape(b, s, n * h)


def alibi_slopes(n):
    return jnp.array([2.0 ** (-8.0 * (i + 1) / n) for i in range(n)], dtype=jnp.float32)


def diff_attention(q, k, v, lam, slopes):
    b, h, _, s, d = q.shape
    nb = s // Q_BLOCK
    scale = d ** -0.5
    q_blocks = jnp.moveaxis(q.reshape(b, h, 2, nb, Q_BLOCK, d), 3, 0)
    starts = jnp.arange(nb, dtype=jnp.int32) * Q_BLOCK
    key_pos = jnp.arange(s, dtype=jnp.int32)

    def one_block(args):
        qb, start = args
        scores = jnp.einsum('bhiqd,bhikd->bhiqk', qb, k).astype(jnp.float32) * scale
        q_pos = start + jnp.arange(Q_BLOCK, dtype=jnp.int32)
        dist = jnp.abs(q_pos[:, None] - key_pos[None, :]).astype(jnp.float32)
        scores = scores - slopes[:, None, None, None] * dist
        p = jax.nn.softmax(scores, axis=-1)
        w = p[:, :, 0] - lam * p[:, :, 1]
        return jnp.einsum('bhqk,bhkv->bhqv', w.astype(v.dtype), v)

    out = lax.map(one_block, (q_blocks, starts))
    return jnp.moveaxis(out, 0, 2).reshape(b, h, s, v.shape[-1])


def chunk_scan(q, k, v, log_a):
    b, h, s, dk = q.shape
    dv = v.shape[-1]
    nc = s // CHUNK

    def to_chunks(t):
        return jnp.moveaxis(t.astype(jnp.float32).reshape(b, h, nc, CHUNK, t.shape[-1]), 2, 0)

    causal_in_chunk = jnp.tril(jnp.ones((CHUNK, CHUNK), dtype=bool))[:, :, None]

    def step(state, inp):
        qc, kc, vc, ac = inp
        cum = jnp.cumsum(ac, axis=2)
        o_inter = jnp.einsum('bhtk,bhkv->bhtv', qc * jnp.exp(cum), state)
        diff = cum[:, :, :, None, :] - cum[:, :, None, :, :]
        decay = jnp.where(causal_in_chunk, jnp.exp(jnp.where(causal_in_chunk, diff, 0.0)), 0.0)
        scores = jnp.einsum('bhtk,bhsk,bhtsk->bhts', qc, kc, decay)
        o_intra = jnp.einsum('bhts,bhsv->bhtv', scores, vc)
        last = cum[:, :, -1:, :]
        new_state = jnp.exp(last[:, :, 0, :])[..., None] * state + jnp.einsum(
            'bhsk,bhsv->bhkv', kc * jnp.exp(last - cum), vc)
        return new_state, o_inter + o_intra

    state0 = jnp.zeros((b, h, dk, dv), jnp.float32)
    _, o = lax.scan(step, state0, (to_chunks(q), to_chunks(k), to_chunks(v), to_chunks(log_a)))
    return jnp.moveaxis(o, 0, 2).reshape(b, h, s, dv)


def bidir_scan(q, k_fwd, k_bwd, v, la_fwd, la_bwd):
    flip = lambda t: jnp.flip(t, axis=2)
    fwd = chunk_scan(q, k_fwd, v, la_fwd)
    bwd = flip(chunk_scan(flip(q), flip(k_bwd), flip(v), flip(la_bwd)))
    return (fwd + bwd).astype(q.dtype)


def mixer_diff_attn(q_cols, k_cols, v_cols, lam_params, subln_g, layer_idx):
    b, s, _ = q_cols.shape

    def qk_heads(t):
        return t.reshape(b, s, DA_HEADS, 2, DA_HEAD_DIM).transpose(0, 2, 3, 1, 4)

    q = qk_heads(q_cols)
    k = qk_heads(k_cols)
    v = to_heads(v_cols, DA_HEADS)
    lam_init = 0.8 - 0.6 * math.exp(-0.3 * layer_idx)
    lp = lam_params.astype(jnp.float32)
    lam = jnp.exp(jnp.sum(lp[0] * lp[1])) - jnp.exp(jnp.sum(lp[2] * lp[3])) + lam_init
    o = diff_attention(q, k, v, lam, alibi_slopes(DA_HEADS))
    o = rms_norm(o, subln_g) * (1.0 - lam_init)
    return from_heads(o)


def mixer_hgrn2(q_cols, f_fwd_cols, f_bwd_cols, i_cols, g_cols, lower_bound, norm_g):
    q = jax.nn.silu(to_heads(q_cols, HB_HEADS)) * HB_HEAD_DIM ** -0.5
    v = to_heads(i_cols, HB_HEADS)

    def forget(cols, lb):
        z = to_heads(cols, HB_HEADS).astype(jnp.float32)
        lbh = lb.reshape(HB_HEADS, 1, HB_HEAD_DIM)
        log_f = jax.nn.log_sigmoid(z) + jnp.log1p(lbh * jnp.exp(jnp.minimum(-z, MAX_NEG_LOGIT)))
        k = (1.0 - lbh) * jax.nn.sigmoid(-z)
        return k, log_f

    k_f, la_f = forget(f_fwd_cols, lower_bound[0])
    k_b, la_b = forget(f_bwd_cols, lower_bound[1])
    o = rms_norm(bidir_scan(q, k_f, k_b, v, la_f, la_b), norm_g)
    return from_heads(o) * jax.nn.silu(g_cols)


def mixer_gla(q_cols, k_cols, v_cols, g_cols, lr_fwd, lr_bwd, gate_w2, gate_b, norm_g):
    q = to_heads(q_cols, GC_HEADS) * GC_KEY_DIM ** -0.5
    k = to_heads(k_cols, GC_HEADS)
    v = to_heads(v_cols, GC_HEADS)

    def log_decay(lr, w2, bias):
        z = (jnp.einsum('bsr,rk->bsk', lr, w2) + bias).astype(jnp.float32)
        return to_heads(jax.nn.log_sigmoid(z) / GC_GATE_NORMALIZER, GC_HEADS)

    la_f = log_decay(lr_fwd, gate_w2[0], gate_b[0])
    la_b = log_decay(lr_bwd, gate_w2[1], gate_b[1])
    o = rms_norm(bidir_scan(q, k, k, v, la_f, la_b), norm_g)
    return from_heads(o) * jax.nn.silu(g_cols)


def setup_inputs(seed: int = 0) -> dict:
    key = jax.random.key(seed)
    ks = jax.random.split(key, 21)
    nrm = lambda k, shape, scale: jax.random.normal(k, shape, jnp.float32) * scale
    gain = lambda k, shape: 1.0 + 0.05 * jax.random.normal(k, shape, jnp.float32)
    return {
        "x": nrm(ks[0], (BATCH, SEQ, D_MODEL), 1.0),
        "c": nrm(ks[1], (BATCH, D_MODEL), 1.0),
        "ada_w": nrm(ks[2], (DEPTH, D_MODEL, N_MOD * D_MODEL), 0.5 * D_MODEL ** -0.5),
        "ada_b": nrm(ks[3], (DEPTH, N_MOD * D_MODEL), 0.02),
        "norm_mix_g": gain(ks[4], (DEPTH, D_MODEL)),
        "norm_mlp_g": gain(ks[5], (DEPTH, D_MODEL)),
        "w_in": nrm(ks[6], (DEPTH, D_MODEL, D_IN), D_MODEL ** -0.5),
        "diff_lambda": nrm(ks[7], (DEPTH, 4, DA_HEAD_DIM), 0.1),
        "diff_subln_g": gain(ks[8], (DEPTH, DA_V_DIM)),
        "hgrn_lb_logits": nrm(ks[9], (DEPTH, 2, HB_WIDTH), 0.1),
        "hgrn_norm_g": gain(ks[10], (DEPTH, HB_HEAD_DIM)),
        "gla_gate_w2": nrm(ks[11], (DEPTH, 2, GC_GATE_RANK, GC_KEY_WIDTH), GC_GATE_RANK ** -0.5),
        "gla_gate_b": nrm(ks[12], (DEPTH, 2, GC_KEY_WIDTH), 0.1),
        "gla_norm_g": gain(ks[13], (DEPTH, GC_VAL_DIM)),
        "w_up_a": nrm(ks[14], (DEPTH, DA_WIDTH, D_MODEL), DA_WIDTH ** -0.5),
        "w_up_b": nrm(ks[15], (DEPTH, HB_WIDTH, D_MODEL), HB_WIDTH ** -0.5),
        "w_up_c": nrm(ks[16], (DEPTH, GC_WIDTH, D_MODEL), GC_WIDTH ** -0.5),
        "w_out": nrm(ks[17], (DEPTH, D_MODEL, D_MODEL), D_MODEL ** -0.5),
        "mlp_w1": nrm(ks[18], (DEPTH, D_MODEL, D_FF), D_MODEL ** -0.5),
        "mlp_w2": nrm(ks[19], (DEPTH, D_FF, D_MODEL), D_FF ** -0.5),
        "final_norm_g": gain(ks[20], (D_MODEL,)),
    }


def reference(x, c, ada_w, ada_b, norm_mix_g, norm_mlp_g, w_in, diff_lambda, diff_subln_g,
              hgrn_lb_logits, hgrn_norm_g, gla_gate_w2, gla_gate_b, gla_norm_g,
              w_up_a, w_up_b, w_up_c, w_out, mlp_w1, mlp_w2, final_norm_g):
    lb_w = jax.nn.softmax(hgrn_lb_logits.astype(jnp.float32), axis=0)
    lower_bounds = jnp.cumsum(lb_w, axis=0) - lb_w[0:1]
    cond = jax.nn.silu(c)
    for l in range(DEPTH):
        mod = cond @ ada_w[l] + ada_b[l]
        sh_m, sc_m, gt_m, sh_f, sc_f, gt_f = [t[:, None, :] for t in jnp.split(mod, N_MOD, axis=-1)]

        h = rms_norm(x, norm_mix_g[l]) * (1.0 + sc_m) + sh_m
        proj = jnp.einsum('bsd,de->bse', h, w_in[l])
        (a_q, a_k, a_v, b_q, b_ff, b_fb, b_i, b_g,
         c_q, c_k, c_v, c_g, c_lrf, c_lrb, gates) = split_cols(proj, IN_SIZES)
        o_a = mixer_diff_attn(a_q, a_k, a_v, diff_lambda[l], diff_subln_g[l], l)
        o_b = mixer_hgrn2(b_q, b_ff, b_fb, b_i, b_g, lower_bounds[l], hgrn_norm_g[l])
        o_c = mixer_gla(c_q, c_k, c_v, c_g, c_lrf, c_lrb, gla_gate_w2[l], gla_gate_b[l], gla_norm_g[l])
        g_a, g_b, g_c = jnp.split(jax.nn.sigmoid(gates), N_BRANCH, axis=-1)
        merged = g_a * (o_a @ w_up_a[l]) + g_b * (o_b @ w_up_b[l]) + g_c * (o_c @ w_up_c[l])
        x = x + gt_m * (merged @ w_out[l])

        h = rms_norm(x, norm_mlp_g[l]) * (1.0 + sc_f) + sh_f
        x = x + gt_f * (jnp.square(jax.nn.relu(h @ mlp_w1[l])) @ mlp_w2[l])
    return rms_norm(x, final_norm_g)
```

```python
import functools
import math

import numpy as np
import jax
import jax.numpy as jnp
from jax import lax
from jax.experimental import pallas as pl
from jax.experimental.pallas import tpu as pltpu

F32 = jnp.float32
BF16 = jnp.bfloat16

D_MODEL = 1024
DEPTH = 4
DA_HEADS = 4
DA_HEAD_DIM = 64
DA_V_DIM = 2 * DA_HEAD_DIM
DA_QK_WIDTH = DA_HEADS * 2 * DA_HEAD_DIM
DA_WIDTH = DA_HEADS * DA_V_DIM
HB_HEADS = 4
HB_HEAD_DIM = 64
HB_WIDTH = HB_HEADS * HB_HEAD_DIM
MAX_NEG_LOGIT = 80.0
GC_HEADS = 4
GC_KEY_DIM = 32
GC_VAL_DIM = 64
GC_KEY_WIDTH = GC_HEADS * GC_KEY_DIM
GC_WIDTH = GC_HEADS * GC_VAL_DIM
GC_GATE_RANK = 16
GC_GATE_NORMALIZER = 16.0
N_BRANCH = 3
D_FF = 4 * D_MODEL
N_MOD = 6
EPS = 1e-6

LANES = 128
SCAN_CHUNK = 64
V7X_VMEM_BYTES = 64 * 1024 * 1024
VMEM_LIMIT = V7X_VMEM_BYTES - 8 * 1024 * 1024


def _dot(a, b):
    return jnp.dot(a, b, preferred_element_type=F32)


def _dot_nt(a, b):
    return lax.dot_general(a, b, (((1,), (1,)), ((), ())), preferred_element_type=F32)


def _dot_tn(a, b):
    return lax.dot_general(a, b, (((0,), (0,)), ((), ())), preferred_element_type=F32)


def _split_bf16(a):
    hi = a.astype(BF16)
    lo = (a - hi.astype(F32)).astype(BF16)
    return hi, lo


def _silu(a):
    return a * jax.nn.sigmoid(a)


def _log_sigmoid(z):
    return -(jnp.maximum(-z, 0.0) + jnp.log1p(jnp.exp(-jnp.abs(z))))


def _params(*sem):
    return pltpu.CompilerParams(dimension_semantics=sem, vmem_limit_bytes=VMEM_LIMIT)


def _const_spec(shape):
    n = len(shape)
    return pl.BlockSpec(shape, lambda *_: (0,) * n, pipeline_mode=pl.Buffered(1))


def _mod_kernel(c_ref, w_ref, b_ref, o_ref):
    cond = _silu(c_ref[...])
    o_ref[0] = jnp.dot(cond, w_ref[0], preferred_element_type=F32,
                       precision=lax.Precision.HIGHEST) + b_ref[0]


def _modulation(c, ada_w, ada_b):
    depth, d, n = ada_w.shape
    b = c.shape[0]
    tn = 1536
    return pl.pallas_call(
        _mod_kernel,
        grid=(depth, n // tn),
        in_specs=[
            pl.BlockSpec((b, d), lambda l, j: (0, 0)),
            pl.BlockSpec((1, d, tn), lambda l, j: (l, 0, j)),
            pl.BlockSpec((1, 1, tn), lambda l, j: (l, 0, j)),
        ],
        out_specs=pl.BlockSpec((1, b, tn), lambda l, j: (l, 0, j)),
        out_shape=jax.ShapeDtypeStruct((depth, b, n), F32),
        compiler_params=_params("arbitrary", "arbitrary"),
        name="modulation",
    )(c, ada_w, ada_b.reshape(depth, 1, n))


def _lb_kernel(z_ref, o_ref):
    z = z_ref[...]
    e = jnp.exp(z - jnp.max(z, axis=0, keepdims=True))
    w = e / jnp.sum(e, axis=0, keepdims=True)
    run = jnp.zeros_like(w[0:1])
    for l in range(z.shape[0]):
        run = run + w[l:l + 1]
        o_ref[l:l + 1, :] = run - w[0:1]


def _lower_bounds(logits):
    depth = logits.shape[0]
    flat = logits.reshape(depth, -1).astype(F32)
    out = pl.pallas_call(
        _lb_kernel,
        out_shape=jax.ShapeDtypeStruct(flat.shape, F32),
        name="hgrn_lower_bounds",
    )(flat)
    return out.reshape(logits.shape)


PROJ_COL_CHUNK = 512


def _pre_kernel(x_ref, sc_ref, sh_ref, g_ref, *refs):
    n = len(refs) // 2
    w_refs, o_refs = refs[:n], refs[n:]
    x = x_ref[0]
    ms = jnp.mean(x * x, axis=-1, keepdims=True)
    y = x * lax.rsqrt(ms + EPS) * g_ref[...]
    h = (y * (1.0 + sc_ref[0]) + sh_ref[0]).astype(BF16)
    for w_ref, o_ref in zip(w_refs, o_refs):
        width = w_ref.shape[1]
        for c0 in range(0, width, PROJ_COL_CHUNK):
            c1 = min(c0 + PROJ_COL_CHUNK, width)
            o_ref[0, :, c0:c1] = _dot(h, w_ref[:, c0:c1]).astype(o_ref.dtype)


def _pre_project(x, sc, sh, g, weights, out_dtypes, tm):
    b, s, d = x.shape
    in_specs = [
        pl.BlockSpec((1, tm, d), lambda i, j: (i, j, 0)),
        pl.BlockSpec((1, 1, d), lambda i, j: (i, 0, 0)),
        pl.BlockSpec((1, 1, d), lambda i, j: (i, 0, 0)),
        _const_spec((1, d)),
    ] + [_const_spec(w.shape) for w in weights]
    out_specs = [pl.BlockSpec((1, tm, w.shape[1]), lambda i, j: (i, j, 0)) for w in weights]
    out_shape = [jax.ShapeDtypeStruct((b, s, w.shape[1]), dt) for w, dt in zip(weights, out_dtypes)]
    return pl.pallas_call(
        _pre_kernel,
        grid=(b, s // tm),
        in_specs=in_specs,
        out_specs=out_specs,
        out_shape=out_shape,
        compiler_params=_params("parallel", "arbitrary"),
        name="prenorm_project",
    )(x, sc, sh, g.reshape(1, d), *weights)


def _attn_kernel(q_ref, k_ref, v_ref, slope_ref, lam_ref, g_ref, o_ref, bias_ref, *, lam_init, tq):
    qi = pl.program_id(0)

    @pl.when(pl.program_id(2) == 0)
    def _():
        s = bias_ref.shape[1]
        row = lax.broadcasted_iota(jnp.int32, (tq, s), 0) + qi * tq
        col = lax.broadcasted_iota(jnp.int32, (tq, s), 1)
        dist = jnp.abs(row - col).astype(F32)
        bias_ref[...] = -(slope_ref[0][:, 0:1] * dist)

    qs = q_ref[0] * jnp.asarray(DA_HEAD_DIM ** -0.5, BF16)
    first = lax.broadcasted_iota(jnp.int32, (1, 2 * DA_HEAD_DIM), 1) < DA_HEAD_DIM
    zero = jnp.zeros_like(qs)
    k = k_ref[0]
    v = v_ref[0]

    def softmax_v(qm):
        sc = _dot_nt(qm, k) + bias_ref[...]
        m = jnp.max(sc, axis=-1, keepdims=True)
        p = jnp.exp(sc - m)
        l = jnp.sum(p, axis=-1, keepdims=True)
        return _dot(p.astype(BF16), v) / l

    lp = lam_ref[...]
    lam = (jnp.exp(jnp.sum(lp[0:1] * lp[1:2], axis=-1, keepdims=True))
           - jnp.exp(jnp.sum(lp[2:3] * lp[3:4], axis=-1, keepdims=True)) + lam_init)
    o = softmax_v(jnp.where(first, qs, zero)) - lam * softmax_v(jnp.where(first, zero, qs))
    y = o * lax.rsqrt(jnp.mean(o * o, axis=-1, keepdims=True) + EPS) * g_ref[...]
    o_ref[0] = (y * (1.0 - lam_init)).astype(o_ref.dtype)


def _diff_attention(pa, lam_params, subln_g, layer_idx, tq):
    b, s, _ = pa.shape
    h = DA_HEADS
    w = DA_V_DIM
    lam_init = 0.8 - 0.6 * math.exp(-0.3 * layer_idx)
    slopes = np.array([2.0 ** (-8.0 * (i + 1) / h) for i in range(h)], np.float32)
    slopes = jnp.asarray(np.broadcast_to(slopes[:, None, None], (h, 1, LANES)).copy())
    return pl.pallas_call(
        functools.partial(_attn_kernel, lam_init=lam_init, tq=tq),
        grid=(s // tq, h, b),
        in_specs=[
            pl.BlockSpec((1, tq, w), lambda qi, hh, bb: (bb, qi, hh)),
            pl.BlockSpec((1, s, w), lambda qi, hh, bb: (bb, 0, h + hh)),
            pl.BlockSpec((1, s, w), lambda qi, hh, bb: (bb, 0, 2 * h + hh)),
            pl.BlockSpec((1, 1, LANES), lambda qi, hh, bb: (hh, 0, 0)),
            pl.BlockSpec(lam_params.shape, lambda qi, hh, bb: (0, 0)),
            pl.BlockSpec((1, w), lambda qi, hh, bb: (0, 0)),
        ],
        out_specs=pl.BlockSpec((1, tq, w), lambda qi, hh, bb: (bb, qi, hh)),
        out_shape=jax.ShapeDtypeStruct((b, s, h * w), BF16),
        scratch_shapes=[pltpu.VMEM((tq, s), F32)],
        compiler_params=_params("arbitrary", "arbitrary", "arbitrary"),
        name="diff_attention",
    )(pa, pa, pa, slopes, lam_params.astype(F32), subln_g.reshape(1, w).astype(F32))


def _scan_constants(c, h, dk, dv):
    n_lev = int(math.log2(c))
    assert 2 ** n_lev == c
    t = np.arange(c)[:, None]
    r = np.arange(c)[None, :]
    blocks = [(r <= t), (r > t)]
    lev = np.full((c, c), -1, np.int32)
    lev[np.arange(c), np.arange(c)] = n_lev
    for li in range(n_lev):
        bs = c >> (li + 1)
        tb, rb = t // bs, r // bs
        blocks.append((tb % 2 == 1) & (rb == tb) & (r <= t))
        blocks.append((tb % 2 == 0) & (rb == tb) & (r > t))
        lev[(tb % 2 == 1) & (rb == tb - 1)] = li
    blocks.append(np.ones((8, c), bool))
    w_f = np.concatenate(blocks, axis=0).astype(np.float32)
    flip = lambda m: m[::-1, ::-1]
    w_b = np.concatenate([flip(m) for m in blocks], axis=0).astype(np.float32)
    head_of_row = np.repeat(np.arange(h), c)[:, None]
    kmask = (head_of_row == (np.arange(h * dk) // dk)[None, :]).astype(np.float32)
    vmask = (head_of_row == (np.arange(h * dv) // dv)[None, :]).astype(np.float32)
    bd = ((np.arange(h * dv) // dv)[:, None] == (np.arange(h * dk) // dk)[None, :]).astype(np.float32)
    bdn = ((np.arange(h * dv) // dv)[:, None] == (np.arange(h * dv) // dv)[None, :]).astype(np.float32)
    return dict(
        w_f=jnp.asarray(w_f, BF16), w_b=jnp.asarray(w_b, BF16),
        lev_f=jnp.asarray(np.tile(lev, (1, h))), lev_b=jnp.asarray(np.tile(flip(lev), (1, h))),
        kmask=jnp.asarray(kmask, BF16), vmask=jnp.asarray(vmask, BF16),
        bd=jnp.asarray(bd), bdn=jnp.asarray(bdn, BF16),
    )


SCAN_CONST_NAMES = ("w_f", "w_b", "lev_f", "lev_b", "kmask", "vmask", "bd", "bdn")


def _scan_chunk(q, k, v, la, st_ref, w_ref, lev_ref, kmask_ref, vmask_ref, bd_ref, n_heads):
    c = q.shape[0]
    n_lev = int(math.log2(c))
    la_hi, la_lo = _split_bf16(la)
    w = w_ref[...]
    ex = jnp.exp(_dot(w, la_hi) + _dot(w, la_lo))
    blk = lambda i: ex[i * c:(i + 1) * c]
    e_tot = ex[(2 + 2 * n_lev) * c:(2 + 2 * n_lev) * c + 1]
    st = st_ref[...]
    kmask = kmask_ref[...]
    lev = lev_ref[...]

    def stack(a, mask):
        return jnp.concatenate([a] * n_heads, axis=0) * mask

    o = _dot_nt((q * blk(0)).astype(BF16), st.astype(BF16))
    acc = jnp.where(lev == n_lev, _dot_nt(q.astype(BF16), stack(k.astype(BF16), kmask)), 0.0)
    for li in range(n_lev):
        ql = (q * blk(2 + 2 * li)).astype(BF16)
        kl = (k * blk(3 + 2 * li)).astype(BF16)
        acc = jnp.where(lev == li, _dot_nt(ql, stack(kl, kmask)), acc)
    o = o + _dot(acc.astype(BF16), stack(v, vmask_ref[...]))
    new = _dot_tn(v, (k * blk(1)).astype(BF16))
    st_ref[...] = st * e_tot + new * bd_ref[...]
    return o


def _head_rmsnorm_gate(o, gain, gate, bdn, dv):
    hi, lo = _split_bf16(o * o)
    ms = (_dot(hi, bdn) + _dot(lo, bdn)) * (1.0 / dv)
    return o * lax.rsqrt(ms + EPS) * gain * _silu(gate)


EPILOGUE_ROWS = 256


def _bidir_scan(prep_f, prep_b, gate_fn, gain_ref, consts, o_ref, of_ref, ob_ref, stf_ref, stb_ref,
                n_heads, dv):
    w_f, w_b, lev_f, lev_b, kmask, vmask, bd, bdn = consts
    s = of_ref.shape[0]
    c = SCAN_CHUNK
    nc = s // c
    stf_ref[...] = jnp.zeros_like(stf_ref)
    stb_ref[...] = jnp.zeros_like(stb_ref)

    def body(j, carry):
        rf = pl.multiple_of(j * c, c)
        rb = pl.multiple_of((nc - 1 - j) * c, c)
        q, k, v, la = prep_f(rf)
        of_ref[pl.ds(rf, c), :] = _scan_chunk(q, k, v, la, stf_ref, w_f, lev_f, kmask, vmask, bd, n_heads)
        q, k, v, la = prep_b(rb)
        ob_ref[pl.ds(rb, c), :] = _scan_chunk(q, k, v, la, stb_ref, w_b, lev_b, kmask, vmask, bd, n_heads)
        return carry

    lax.fori_loop(0, nc, body, 0)
    rows = min(EPILOGUE_ROWS, s)
    for r0 in range(0, s, rows):
        o = of_ref[r0:r0 + rows, :] + ob_ref[r0:r0 + rows, :]
        y = _head_rmsnorm_gate(o, gain_ref[...], gate_fn(r0, rows), bdn[...], dv)
        o_ref[0, r0:r0 + rows, :] = y.astype(o_ref.dtype)


def _hgrn_kernel(pb_ref, pf_ref, lb_ref, gain_ref, *refs):
    consts, (o_ref, of_ref, ob_ref, stf_ref, stb_ref) = refs[:8], refs[8:]
    c = SCAN_CHUNK
    wd = HB_WIDTH

    def prep(direction):
        lb = lb_ref[direction:direction + 1, :]

        def fn(r0):
            qiv = pb_ref[0, pl.ds(r0, c), :]
            q = _silu(qiv[:, 0:wd].astype(F32)) * HB_HEAD_DIM ** -0.5
            v = qiv[:, wd:2 * wd]
            z = pf_ref[0, pl.ds(r0, c), direction * wd:(direction + 1) * wd]
            la = _log_sigmoid(z) + jnp.log1p(lb * jnp.exp(jnp.minimum(-z, MAX_NEG_LOGIT)))
            k = (1.0 - lb) * jax.nn.sigmoid(-z)
            return q, k, v, la
        return fn

    gate_fn = lambda r0, rows: pb_ref[0, r0:r0 + rows, 2 * wd:3 * wd].astype(F32)
    _bidir_scan(prep(0), prep(1), gate_fn, gain_ref, consts, o_ref, of_ref, ob_ref, stf_ref, stb_ref,
                HB_HEADS, HB_HEAD_DIM)


def _gla_kernel(pc_ref, plr_ref, w2h_ref, w2l_ref, gb_ref, gain_ref, *refs):
    consts, (o_ref, of_ref, ob_ref, stf_ref, stb_ref) = refs[:8], refs[8:]
    c = SCAN_CHUNK
    kw = GC_KEY_WIDTH
    vw = GC_WIDTH

    def prep(direction):
        def fn(r0):
            qkv = pc_ref[0, pl.ds(r0, c), :]
            q = qkv[:, 0:kw].astype(F32) * GC_KEY_DIM ** -0.5
            k = qkv[:, kw:2 * kw].astype(F32)
            v = qkv[:, 2 * kw:2 * kw + vw]
            lr_hi, lr_lo = _split_bf16(plr_ref[0, pl.ds(r0, c), :])
            w2h = w2h_ref[:, direction * kw:(direction + 1) * kw]
            w2l = w2l_ref[:, direction * kw:(direction + 1) * kw]
            z = (_dot(lr_hi, w2h) + _dot(lr_lo, w2h) + _dot(lr_hi, w2l)
                 + gb_ref[:, direction * kw:(direction + 1) * kw])
            la = _log_sigmoid(z) / GC_GATE_NORMALIZER
            return q, k, v, la
        return fn

    gate_fn = lambda r0, rows: pc_ref[0, r0:r0 + rows, 2 * kw + vw:2 * kw + 2 * vw].astype(F32)
    _bidir_scan(prep(0), prep(1), gate_fn, gain_ref, consts, o_ref, of_ref, ob_ref, stf_ref, stb_ref,
                GC_HEADS, GC_VAL_DIM)


def _scan_call(kernel, name, seq_inputs, small_inputs, consts, b, s, hk, hv):
    const_arrays = [consts[n] for n in SCAN_CONST_NAMES]
    in_specs = ([pl.BlockSpec((1, s, a.shape[2]), lambda i: (i, 0, 0)) for a in seq_inputs]
                + [_const_spec(a.shape) for a in small_inputs]
                + [_const_spec(a.shape) for a in const_arrays])
    return pl.pallas_call(
        kernel,
        grid=(b,),
        in_specs=in_specs,
        out_specs=pl.BlockSpec((1, s, hv), lambda i: (i, 0, 0)),
        out_shape=jax.ShapeDtypeStruct((b, s, hv), BF16),
        scratch_shapes=[pltpu.VMEM((s, hv), F32), pltpu.VMEM((s, hv), F32),
                        pltpu.VMEM((hv, hk), F32), pltpu.VMEM((hv, hk), F32)],
        compiler_params=_params("parallel"),
        name=name,
    )(*seq_inputs, *small_inputs, *const_arrays)


FF_CHUNK = 1024


def _post_kernel(x_ref, oa_ref, ob_ref, oc_ref, pg_ref, gtm_ref, scf_ref, shf_ref, gtf_ref, g2_ref,
                 wua_ref, wub_ref, wuc_ref, wo_ref, w1_ref, w2_ref, fg_ref, out_ref, *, final):
    d = x_ref.shape[2]
    gate = lambda i: jax.nn.sigmoid(pg_ref[0, :, i * d:(i + 1) * d].astype(F32))
    merged = (gate(0) * _dot(oa_ref[0], wua_ref[...])
              + gate(1) * _dot(ob_ref[0], wub_ref[...])
              + gate(2) * _dot(oc_ref[0], wuc_ref[...]))
    x1 = x_ref[0] + gtm_ref[0] * _dot(merged.astype(BF16), wo_ref[...])
    ms = jnp.mean(x1 * x1, axis=-1, keepdims=True)
    y = x1 * lax.rsqrt(ms + EPS) * g2_ref[...]
    h = (y * (1.0 + scf_ref[0]) + shf_ref[0]).astype(BF16)
    acc = jnp.zeros_like(x1)
    for c0 in range(0, w1_ref.shape[1], FF_CHUNK):
        u = jnp.maximum(_dot(h, w1_ref[:, c0:c0 + FF_CHUNK]), 0.0)
        acc = acc + _dot((u * u).astype(BF16), w2_ref[c0:c0 + FF_CHUNK, :])
    x2 = x1 + gtf_ref[0] * acc
    if final:
        x2 = x2 * lax.rsqrt(jnp.mean(x2 * x2, axis=-1, keepdims=True) + EPS) * fg_ref[...]
    out_ref[0] = x2


def _post_mix_mlp(x, oa, ob, oc, pg, gtm, scf, shf, gtf, g2, weights, fg, final, tm):
    b, s, d = x.shape
    tok = lambda a: pl.BlockSpec((1, tm, a.shape[2]), lambda i, j: (i, j, 0))
    vec = pl.BlockSpec((1, 1, d), lambda i, j: (i, 0, 0))
    return pl.pallas_call(
        functools.partial(_post_kernel, final=final),
        grid=(b, s // tm),
        in_specs=[tok(x), tok(oa), tok(ob), tok(oc), tok(pg), vec, vec, vec, vec, _const_spec((1, d))]
        + [_const_spec(w.shape) for w in weights] + [_const_spec((1, d))],
        out_specs=pl.BlockSpec((1, tm, d), lambda i, j: (i, j, 0)),
        out_shape=jax.ShapeDtypeStruct((b, s, d), F32),
        compiler_params=_params("parallel", "arbitrary"),
        name="merge_out_mlp",
    )(x, oa, ob, oc, pg, gtm, scf, shf, gtf, g2.reshape(1, d), *weights, fg.reshape(1, d))


def _input_projection_groups(w):
    o_b = 2 * DA_QK_WIDTH + DA_WIDTH
    o_c = o_b + 5 * HB_WIDTH
    o_lr = o_c + 2 * GC_KEY_WIDTH + 2 * GC_WIDTH
    o_g = o_lr + 2 * GC_GATE_RANK
    wb = HB_WIDTH
    attn = w[:, :o_b]
    hgrn_qig = jnp.concatenate([w[:, o_b:o_b + wb], w[:, o_b + 3 * wb:o_b + 5 * wb]], axis=1)
    hgrn_f = w[:, o_b + wb:o_b + 3 * wb]
    gla = w[:, o_c:o_lr]
    lr = jnp.pad(w[:, o_lr:o_g], ((0, 0), (0, LANES - 2 * GC_GATE_RANK)))
    gates = w[:, o_g:]
    return [g.astype(BF16) for g in (attn, hgrn_qig, hgrn_f, gla, lr, gates)]


PROJ_OUT_DTYPES = (BF16, BF16, F32, BF16, F32, BF16)


def kernel(x, c, ada_w, ada_b, norm_mix_g, norm_mlp_g, w_in, diff_lambda, diff_subln_g, hgrn_lb_logits,
           hgrn_norm_g, gla_gate_w2, gla_gate_b, gla_norm_g, w_up_a, w_up_b, w_up_c, w_out, mlp_w1, mlp_w2,
           final_norm_g):
    b, s, d = x.shape
    tm = min(512, s)
    tq = min(256, s)
    mod = _modulation(c, ada_w, ada_b).reshape(DEPTH, b, N_MOD, 1, d)
    lower_bounds = _lower_bounds(hgrn_lb_logits)
    hgrn_consts = _scan_constants(SCAN_CHUNK, HB_HEADS, HB_HEAD_DIM, HB_HEAD_DIM)
    gla_consts = _scan_constants(SCAN_CHUNK, GC_HEADS, GC_KEY_DIM, GC_VAL_DIM)
    for l in range(DEPTH):
        sh_m, sc_m, gt_m, sh_f, sc_f, gt_f = [mod[l, :, i] for i in range(N_MOD)]
        pa, pb, pf, pc, plr, pg = _pre_project(
            x, sc_m, sh_m, norm_mix_g[l], _input_projection_groups(w_in[l]), PROJ_OUT_DTYPES, tm)
        o_a = _diff_attention(pa, diff_lambda[l], diff_subln_g[l], l, tq)
        o_b = _scan_call(
            _hgrn_kernel, "hgrn2_scan", [pb, pf],
            [lower_bounds[l], jnp.tile(hgrn_norm_g[l], HB_HEADS).reshape(1, HB_WIDTH)],
            hgrn_consts, b, s, HB_WIDTH, HB_WIDTH)
        w2 = jnp.zeros((LANES, 2 * GC_KEY_WIDTH), F32)
        w2 = w2.at[0:GC_GATE_RANK, 0:GC_KEY_WIDTH].set(gla_gate_w2[l, 0])
        w2 = w2.at[GC_GATE_RANK:2 * GC_GATE_RANK, GC_KEY_WIDTH:].set(gla_gate_w2[l, 1])
        w2_hi = w2.astype(BF16)
        w2_lo = (w2 - w2_hi.astype(F32)).astype(BF16)
        o_c = _scan_call(
            _gla_kernel, "gla_scan", [pc, plr],
            [w2_hi, w2_lo, gla_gate_b[l].reshape(1, 2 * GC_KEY_WIDTH),
             jnp.tile(gla_norm_g[l], GC_HEADS).reshape(1, GC_WIDTH)],
            gla_consts, b, s, GC_KEY_WIDTH, GC_WIDTH)
        weights = [w.astype(BF16) for w in (w_up_a[l], w_up_b[l], w_up_c[l], w_out[l], mlp_w1[l], mlp_w2[l])]
        x = _post_mix_mlp(x, o_a, o_b, o_c, pg, gt_m, sc_f, sh_f, gt_f, norm_mlp_g[l], weights,
                          final_norm_g, l == DEPTH - 1, tm)
    return x
```

```python
import functools
import math

import numpy as np
import jax
import jax.numpy as jnp
from jax import lax
from jax.experimental import pallas as pl
from jax.experimental.pallas import tpu as pltpu

F32 = jnp.float32
BF16 = jnp.bfloat16

D_MODEL = 1024
DEPTH = 4
DA_HEADS = 4
DA_HEAD_DIM = 64
DA_V_DIM = 2 * DA_HEAD_DIM
DA_QK_WIDTH = DA_HEADS * 2 * DA_HEAD_DIM
DA_WIDTH = DA_HEADS * DA_V_DIM
HB_HEADS = 4
HB_HEAD_DIM = 64
HB_WIDTH = HB_HEADS * HB_HEAD_DIM
MAX_NEG_LOGIT = 80.0
GC_HEADS = 4
GC_KEY_DIM = 32
GC_VAL_DIM = 64
GC_KEY_WIDTH = GC_HEADS * GC_KEY_DIM
GC_WIDTH = GC_HEADS * GC_VAL_DIM
GC_GATE_RANK = 16
GC_GATE_NORMALIZER = 16.0
N_BRANCH = 3
D_FF = 4 * D_MODEL
N_MOD = 6
EPS = 1e-6
LOG2_E = 1.4426950408889634
ATTN_SUB_ROWS = 256

LANES = 128
SCAN_CHUNK = 64
HGRN_CHUNKS_PER_STEP = 2
GLA_CHUNKS_PER_STEP = 4
V7X_VMEM_BYTES = 64 * 1024 * 1024
VMEM_LIMIT = V7X_VMEM_BYTES - 8 * 1024 * 1024


def _dot(a, b):
    return jnp.dot(a, b, preferred_element_type=F32)


def _dot_nt(a, b):
    return lax.dot_general(a, b, (((1,), (1,)), ((), ())), preferred_element_type=F32)


def _dot_tn(a, b):
    return lax.dot_general(a, b, (((0,), (0,)), ((), ())), preferred_element_type=F32)


def _split_bf16(a):
    hi = a.astype(BF16)
    lo = (a - hi.astype(F32)).astype(BF16)
    return hi, lo


def _sigmoid(a):
    return 1.0 / (1.0 + jnp.exp(-a))


def _silu(a):
    return a * _sigmoid(a)


def _log_sigmoid(z):
    return -(jnp.maximum(-z, 0.0) + jnp.log(1.0 + jnp.exp(-jnp.abs(z))))


def _params(*sem):
    return pltpu.CompilerParams(dimension_semantics=sem, vmem_limit_bytes=VMEM_LIMIT)


def _const_spec(shape):
    n = len(shape)
    return pl.BlockSpec(shape, lambda *_: (0,) * n, pipeline_mode=pl.Buffered(1))


def _mod_kernel(c_ref, w_ref, b_ref, o_ref):
    cond = _silu(c_ref[...])
    o_ref[0] = jnp.dot(cond, w_ref[0], preferred_element_type=F32,
                       precision=lax.Precision.HIGHEST) + b_ref[0]


def _modulation(c, ada_w, ada_b):
    depth, d, n = ada_w.shape
    b = c.shape[0]
    tn = 1536
    return pl.pallas_call(
        _mod_kernel,
        grid=(depth, n // tn),
        in_specs=[
            pl.BlockSpec((b, d), lambda l, j: (0, 0)),
            pl.BlockSpec((1, d, tn), lambda l, j: (l, 0, j)),
            pl.BlockSpec((1, 1, tn), lambda l, j: (l, 0, j)),
        ],
        out_specs=pl.BlockSpec((1, b, tn), lambda l, j: (l, 0, j)),
        out_shape=jax.ShapeDtypeStruct((depth, b, n), F32),
        compiler_params=_params("arbitrary", "arbitrary"),
        name="modulation",
    )(c, ada_w, ada_b.reshape(depth, 1, n))


def _lb_kernel(z_ref, o_ref):
    z = z_ref[...]
    e = jnp.exp(z - jnp.max(z, axis=0, keepdims=True))
    w = e / jnp.sum(e, axis=0, keepdims=True)
    run = jnp.zeros_like(w[0:1])
    for l in range(z.shape[0]):
        run = run + w[l:l + 1]
        o_ref[l:l + 1, :] = run - w[0:1]


def _lower_bounds(logits):
    depth = logits.shape[0]
    flat = logits.reshape(depth, -1).astype(F32)
    out = pl.pallas_call(
        _lb_kernel,
        out_shape=jax.ShapeDtypeStruct(flat.shape, F32),
        name="hgrn_lower_bounds",
    )(flat)
    return out.reshape(logits.shape)


PROJ_COL_CHUNK = 512


def _pre_kernel(x_ref, sc_ref, sh_ref, g_ref, *refs):
    n = len(refs) // 2
    w_refs, o_refs = refs[:n], refs[n:]
    x = x_ref[0]
    ms = jnp.mean(x * x, axis=-1, keepdims=True)
    y = x * lax.rsqrt(ms + EPS) * g_ref[...]
    h = (y * (1.0 + sc_ref[0]) + sh_ref[0]).astype(BF16)
    for w_ref, o_ref in zip(w_refs, o_refs):
        width = w_ref.shape[1]
        for c0 in range(0, width, PROJ_COL_CHUNK):
            c1 = min(c0 + PROJ_COL_CHUNK, width)
            o_ref[0, :, c0:c1] = _dot(h, w_ref[:, c0:c1]).astype(o_ref.dtype)


def _pre_project(x, sc, sh, g, weights, out_dtypes, tm):
    b, s, d = x.shape
    in_specs = [
        pl.BlockSpec((1, tm, d), lambda i, j: (i, j, 0)),
        pl.BlockSpec((1, 1, d), lambda i, j: (i, 0, 0)),
        pl.BlockSpec((1, 1, d), lambda i, j: (i, 0, 0)),
        _const_spec((1, d)),
    ] + [_const_spec(w.shape) for w in weights]
    out_specs = [pl.BlockSpec((1, tm, w.shape[1]), lambda i, j: (i, j, 0)) for w in weights]
    out_shape = [jax.ShapeDtypeStruct((b, s, w.shape[1]), dt) for w, dt in zip(weights, out_dtypes)]
    return pl.pallas_call(
        _pre_kernel,
        grid=(b, s // tm),
        in_specs=in_specs,
        out_specs=out_specs,
        out_shape=out_shape,
        compiler_params=_params("parallel", "arbitrary"),
        name="prenorm_project",
    )(x, sc, sh, g.reshape(1, d), *weights)


def _attn_kernel(q_ref, k_ref, v_ref, slope_ref, lam_ref, g_ref, o_ref, bias_ref, *, lam_init, tq):
    qi = pl.program_id(0)

    @pl.when(pl.program_id(2) == 0)
    def _():
        s = bias_ref.shape[1]
        row = lax.broadcasted_iota(jnp.int32, (tq, s), 0) + qi * tq
        col = lax.broadcasted_iota(jnp.int32, (tq, s), 1)
        dist = jnp.abs(row - col).astype(F32)
        bias_ref[...] = -(slope_ref[0][:, 0:1] * LOG2_E) * dist

    first = lax.broadcasted_iota(jnp.int32, (1, 2 * DA_HEAD_DIM), 1) < DA_HEAD_DIM
    k = k_ref[0]
    v = v_ref[0]
    sub = min(ATTN_SUB_ROWS, tq)

    def scores(r0):
        qs = (q_ref[0, r0:r0 + sub, :].astype(F32) * (DA_HEAD_DIM ** -0.5 * LOG2_E)).astype(BF16)
        zero = jnp.zeros_like(qs)
        bias = bias_ref[r0:r0 + sub, :]
        return (_dot_nt(jnp.where(first, qs, zero), k) + bias,
                _dot_nt(jnp.where(first, zero, qs), k) + bias)

    def softmax_v(sc):
        m = jnp.max(sc, axis=-1, keepdims=True)
        p = jnp.exp2(sc - m)
        l = jnp.sum(p, axis=-1, keepdims=True)
        return _dot(p.astype(BF16), v) / l

    lp = lam_ref[...]
    lam = (jnp.exp(jnp.sum(lp[0:1] * lp[1:2], axis=-1, keepdims=True))
           - jnp.exp(jnp.sum(lp[2:3] * lp[3:4], axis=-1, keepdims=True)) + lam_init)
    nxt = scores(0)
    for r0 in range(0, tq, sub):
        sc1, sc2 = nxt
        if r0 + sub < tq:
            nxt = scores(r0 + sub)
        o = softmax_v(sc1) - lam * softmax_v(sc2)
        y = o * lax.rsqrt(jnp.mean(o * o, axis=-1, keepdims=True) + EPS) * g_ref[...]
        o_ref[0, r0:r0 + sub, :] = (y * (1.0 - lam_init)).astype(o_ref.dtype)


def _diff_attention(pa, lam_params, subln_g, layer_idx, tq):
    b, s, _ = pa.shape
    h = DA_HEADS
    w = DA_V_DIM
    lam_init = 0.8 - 0.6 * math.exp(-0.3 * layer_idx)
    slopes = np.array([2.0 ** (-8.0 * (i + 1) / h) for i in range(h)], np.float32)
    slopes = jnp.asarray(np.broadcast_to(slopes[:, None, None], (h, 1, LANES)).copy())
    return pl.pallas_call(
        functools.partial(_attn_kernel, lam_init=lam_init, tq=tq),
        grid=(s // tq, h, b),
        in_specs=[
            pl.BlockSpec((1, tq, w), lambda qi, hh, bb: (bb, qi, hh)),
            pl.BlockSpec((1, s, w), lambda qi, hh, bb: (bb, 0, h + hh)),
            pl.BlockSpec((1, s, w), lambda qi, hh, bb: (bb, 0, 2 * h + hh)),
            pl.BlockSpec((1, 1, LANES), lambda qi, hh, bb: (hh, 0, 0)),
            pl.BlockSpec(lam_params.shape, lambda qi, hh, bb: (0, 0)),
            pl.BlockSpec((1, w), lambda qi, hh, bb: (0, 0)),
        ],
        out_specs=pl.BlockSpec((1, tq, w), lambda qi, hh, bb: (bb, qi, hh)),
        out_shape=jax.ShapeDtypeStruct((b, s, h * w), BF16),
        scratch_shapes=[pltpu.VMEM((tq, s), F32)],
        compiler_params=_params("arbitrary", "arbitrary", "arbitrary"),
        name="diff_attention",
    )(pa, pa, pa, slopes, lam_params.astype(F32), subln_g.reshape(1, w).astype(F32))


def _scan_constants(c, h, dk, dv):
    n_lev = int(math.log2(c))
    assert 2 ** n_lev == c
    t = np.arange(c)[:, None]
    r = np.arange(c)[None, :]
    blocks = [(r <= t), (r > t)]
    lev = np.full((c, c), -1, np.int32)
    lev[np.arange(c), np.arange(c)] = n_lev
    for li in range(n_lev):
        bs = c >> (li + 1)
        tb, rb = t // bs, r // bs
        blocks.append((rb == tb) & np.where(tb % 2 == 1, r <= t, r > t))
        lev[(tb % 2 == 1) & (rb == tb - 1)] = li
    blocks.append(np.ones((8, c), bool))
    flip = lambda m: m[::-1, ::-1]
    twice = lambda ms: np.tile(np.concatenate(ms, axis=0).astype(np.float32), (1, 2))
    w_f = twice(blocks)
    w_b = twice([flip(m) for m in blocks])
    head_of_row = np.repeat(np.arange(h), c)[:, None]
    kmask = (head_of_row == (np.arange(h * dk) // dk)[None, :]).astype(np.float32).T
    vmask = (head_of_row == (np.arange(h * dv) // dv)[None, :]).astype(np.float32)
    bd = ((np.arange(h * dv) // dv)[:, None] == (np.arange(h * dk) // dk)[None, :]).astype(np.float32)
    bdn = ((np.arange(h * dv) // dv)[:, None] == (np.arange(h * dv) // dv)[None, :]).astype(np.float32)
    return dict(
        w_f=jnp.asarray(w_f, BF16), w_b=jnp.asarray(w_b, BF16),
        lev_f=jnp.asarray(np.tile(lev, (1, h))), lev_b=jnp.asarray(np.tile(flip(lev), (1, h))),
        kmask=jnp.asarray(kmask, BF16), vmask=jnp.asarray(vmask, BF16),
        bd=jnp.asarray(bd), bdn=jnp.asarray(bdn, BF16),
    )


SCAN_CONST_NAMES = ("w_f", "w_b", "lev_f", "lev_b", "kmask", "vmask", "bd", "bdn")


def _scan_chunks(chains, kmask_ref, vmask_ref, bd_ref, n_heads):
    c = chains[0][0].shape[0]
    n_lev = int(math.log2(c))
    kmask = kmask_ref[...]
    vmask = vmask_ref[...]
    blk = lambda ex, i: ex[i * c:(i + 1) * c]

    def scores(qa, ka):
        kt = jnp.concatenate([ka, ka], axis=0).T
        return _dot(qa, jnp.concatenate([kt] * (n_heads // 2), axis=1) * kmask)

    exs = []
    for q, k, v, la, st_ref, w_ref, lev_ref in chains:
        la_hi, la_lo = _split_bf16(la)
        exs.append(jnp.exp(_dot(w_ref[...], jnp.concatenate([la_hi, la_lo], axis=0))))
    accs = [jnp.where(ch[6][...] == n_lev, scores(ch[0].astype(BF16), ch[1].astype(BF16)), 0.0)
            for ch in chains]
    for li in range(n_lev):
        for i, (ch, ex) in enumerate(zip(chains, exs)):
            e = blk(ex, 2 + li)
            r = scores((ch[0] * e).astype(BF16), (ch[1] * e).astype(BF16))
            accs[i] = jnp.where(ch[6][...] == li, r, accs[i])
    outs = []
    for ch, acc in zip(chains, accs):
        v_stack = jnp.concatenate([ch[2]] * n_heads, axis=0) * vmask
        outs.append(_dot(acc.astype(BF16), v_stack))
    news = [_dot_tn(ch[2], (ch[1] * blk(ex, 1)).astype(BF16)) for ch, ex in zip(chains, exs)]
    for i, (ch, ex) in enumerate(zip(chains, exs)):
        st_ref = ch[4]
        st = st_ref[...]
        outs[i] = outs[i] + _dot_nt((ch[0] * blk(ex, 0)).astype(BF16), st.astype(BF16))
        e_tot = ex[(2 + n_lev) * c:(2 + n_lev) * c + 1]
        st_ref[...] = st * e_tot + news[i] * bd_ref[...]
    return outs


def _head_rmsnorm_gate(o, gain, gate, bdn, dv):
    hi, lo = _split_bf16(o * o)
    ms = (_dot(hi, bdn) + _dot(lo, bdn)) * (1.0 / dv)
    return o * lax.rsqrt(ms + EPS) * gain * _silu(gate)


EPILOGUE_ROWS = 256


def _bidir_scan(prep_f, prep_b, gate_fn, gain_ref, consts, o_ref, of_ref, ob_ref, stf_ref, stb_ref,
                n_heads, dv, group):
    w_f, w_b, lev_f, lev_b, kmask, vmask, bd, bdn = consts
    s = of_ref.shape[0]
    c = SCAN_CHUNK
    nc = s // c
    stf_ref[...] = jnp.zeros_like(stf_ref)
    stb_ref[...] = jnp.zeros_like(stb_ref)

    assert nc % group == 0

    def body(j, carry):
        rows, chains = [], []
        for g in range(group):
            rf = pl.multiple_of((j * group + g) * c, c)
            rb = pl.multiple_of((nc - 1 - j * group - g) * c, c)
            rows += [(of_ref, rf), (ob_ref, rb)]
            chains += [prep_f(rf) + (stf_ref, w_f, lev_f), prep_b(rb) + (stb_ref, w_b, lev_b)]
        outs = _scan_chunks(chains, kmask, vmask, bd, n_heads)
        for (ref, r0), o in zip(rows, outs):
            ref[pl.ds(r0, c), :] = o
        return carry

    lax.fori_loop(0, nc // group, body, 0)
    rows = min(EPILOGUE_ROWS, s)
    for r0 in range(0, s, rows):
        o = of_ref[r0:r0 + rows, :] + ob_ref[r0:r0 + rows, :]
        y = _head_rmsnorm_gate(o, gain_ref[...], gate_fn(r0, rows), bdn[...], dv)
        o_ref[0, r0:r0 + rows, :] = y.astype(o_ref.dtype)


def _hgrn_kernel(pb_ref, pf_ref, lb_ref, gain_ref, *refs):
    consts, (o_ref, of_ref, ob_ref, stf_ref, stb_ref) = refs[:8], refs[8:]
    c = SCAN_CHUNK
    wd = HB_WIDTH

    def prep(direction):
        lb = lb_ref[direction:direction + 1, :]

        def fn(r0):
            qiv = pb_ref[0, pl.ds(r0, c), :]
            q = _silu(qiv[:, 0:wd].astype(F32)) * HB_HEAD_DIM ** -0.5
            v = qiv[:, wd:2 * wd]
            z = pf_ref[0, pl.ds(r0, c), direction * wd:(direction + 1) * wd]
            e = jnp.exp(-jnp.abs(z))
            r = 1.0 / (1.0 + e)
            la = (-(jnp.maximum(-z, 0.0) + jnp.log(1.0 + e))
                  + jnp.log(1.0 + lb * jnp.exp(jnp.minimum(-z, MAX_NEG_LOGIT))))
            k = (1.0 - lb) * jnp.where(z >= 0.0, e * r, r)
            return q, k, v, la
        return fn

    gate_fn = lambda r0, rows: pb_ref[0, r0:r0 + rows, 2 * wd:3 * wd].astype(F32)
    _bidir_scan(prep(0), prep(1), gate_fn, gain_ref, consts, o_ref, of_ref, ob_ref, stf_ref, stb_ref,
                HB_HEADS, HB_HEAD_DIM, HGRN_CHUNKS_PER_STEP)


def _gla_kernel(pc_ref, plr_ref, w2h_ref, w2l_ref, gb_ref, gain_ref, *refs):
    consts, (o_ref, of_ref, ob_ref, stf_ref, stb_ref) = refs[:8], refs[8:]
    c = SCAN_CHUNK
    kw = GC_KEY_WIDTH
    vw = GC_WIDTH

    def prep(direction):
        def fn(r0):
            qkv = pc_ref[0, pl.ds(r0, c), :]
            q = qkv[:, 0:kw].astype(F32) * GC_KEY_DIM ** -0.5
            k = qkv[:, kw:2 * kw].astype(F32)
            v = qkv[:, 2 * kw:2 * kw + vw]
            lr_hi, lr_lo = _split_bf16(plr_ref[0, pl.ds(r0, c), :])
            w2h = w2h_ref[:, direction * kw:(direction + 1) * kw]
            w2l = w2l_ref[:, direction * kw:(direction + 1) * kw]
            z = (_dot(lr_hi, w2h) + _dot(lr_lo, w2h) + _dot(lr_hi, w2l)
                 + gb_ref[:, direction * kw:(direction + 1) * kw])
            la = _log_sigmoid(z) / GC_GATE_NORMALIZER
            return q, k, v, la
        return fn

    gate_fn = lambda r0, rows: pc_ref[0, r0:r0 + rows, 2 * kw + vw:2 * kw + 2 * vw].astype(F32)
    _bidir_scan(prep(0), prep(1), gate_fn, gain_ref, consts, o_ref, of_ref, ob_ref, stf_ref, stb_ref,
                GC_HEADS, GC_VAL_DIM, GLA_CHUNKS_PER_STEP)


def _scan_call(kernel, name, seq_inputs, small_inputs, consts, b, s, hk, hv):
    const_arrays = [consts[n] for n in SCAN_CONST_NAMES]
    in_specs = ([pl.BlockSpec((1, s, a.shape[2]), lambda i: (i, 0, 0)) for a in seq_inputs]
                + [_const_spec(a.shape) for a in small_inputs]
                + [_const_spec(a.shape) for a in const_arrays])
    return pl.pallas_call(
        kernel,
        grid=(b,),
        in_specs=in_specs,
        out_specs=pl.BlockSpec((1, s, hv), lambda i: (i, 0, 0)),
        out_shape=jax.ShapeDtypeStruct((b, s, hv), BF16),
        scratch_shapes=[pltpu.VMEM((s, hv), F32), pltpu.VMEM((s, hv), F32),
                        pltpu.VMEM((hv, hk), F32), pltpu.VMEM((hv, hk), F32)],
        compiler_params=_params("parallel"),
        name=name,
    )(*seq_inputs, *small_inputs, *const_arrays)


FF_CHUNK = 1024


def _post_kernel(x_ref, oa_ref, ob_ref, oc_ref, pg_ref, gtm_ref, scf_ref, shf_ref, gtf_ref, g2_ref,
                 wua_ref, wub_ref, wuc_ref, wo_ref, w1_ref, w2_ref, fg_ref, out_ref, *, final):
    d = x_ref.shape[2]
    gate = lambda i: _sigmoid(pg_ref[0, :, i * d:(i + 1) * d].astype(F32))
    merged = (gate(0) * _dot(oa_ref[0], wua_ref[...])
              + gate(1) * _dot(ob_ref[0], wub_ref[...])
              + gate(2) * _dot(oc_ref[0], wuc_ref[...]))
    x1 = x_ref[0] + gtm_ref[0] * _dot(merged.astype(BF16), wo_ref[...])
    ms = jnp.mean(x1 * x1, axis=-1, keepdims=True)
    y = x1 * lax.rsqrt(ms + EPS) * g2_ref[...]
    h = (y * (1.0 + scf_ref[0]) + shf_ref[0]).astype(BF16)
    acc = jnp.zeros_like(x1)
    for c0 in range(0, w1_ref.shape[1], FF_CHUNK):
        u = jnp.maximum(_dot(h, w1_ref[:, c0:c0 + FF_CHUNK]), 0.0)
        acc = acc + _dot((u * u).astype(BF16), w2_ref[c0:c0 + FF_CHUNK, :])
    x2 = x1 + gtf_ref[0] * acc
    if final:
        x2 = x2 * lax.rsqrt(jnp.mean(x2 * x2, axis=-1, keepdims=True) + EPS) * fg_ref[...]
    out_ref[0] = x2


def _post_mix_mlp(x, oa, ob, oc, pg, gtm, scf, shf, gtf, g2, weights, fg, final, tm):
    b, s, d = x.shape
    tok = lambda a: pl.BlockSpec((1, tm, a.shape[2]), lambda i, j: (i, j, 0))
    vec = pl.BlockSpec((1, 1, d), lambda i, j: (i, 0, 0))
    return pl.pallas_call(
        functools.partial(_post_kernel, final=final),
        grid=(b, s // tm),
        in_specs=[tok(x), tok(oa), tok(ob), tok(oc), tok(pg), vec, vec, vec, vec, _const_spec((1, d))]
        + [_const_spec(w.shape) for w in weights] + [_const_spec((1, d))],
        out_specs=pl.BlockSpec((1, tm, d), lambda i, j: (i, j, 0)),
        out_shape=jax.ShapeDtypeStruct((b, s, d), F32),
        compiler_params=_params("parallel", "arbitrary"),
        name="merge_out_mlp",
    )(x, oa, ob, oc, pg, gtm, scf, shf, gtf, g2.reshape(1, d), *weights, fg.reshape(1, d))


def _input_projection_groups(w):
    o_b = 2 * DA_QK_WIDTH + DA_WIDTH
    o_c = o_b + 5 * HB_WIDTH
    o_lr = o_c + 2 * GC_KEY_WIDTH + 2 * GC_WIDTH
    o_g = o_lr + 2 * GC_GATE_RANK
    wb = HB_WIDTH
    attn = w[:, :o_b]
    hgrn_qig = jnp.concatenate([w[:, o_b:o_b + wb], w[:, o_b + 3 * wb:o_b + 5 * wb]], axis=1)
    hgrn_f = w[:, o_b + wb:o_b + 3 * wb]
    gla = w[:, o_c:o_lr]
    lr = jnp.pad(w[:, o_lr:o_g], ((0, 0), (0, LANES - 2 * GC_GATE_RANK)))
    gates = w[:, o_g:]
    return [g.astype(BF16) for g in (attn, hgrn_qig, hgrn_f, gla, lr, gates)]


PROJ_OUT_DTYPES = (BF16, BF16, F32, BF16, F32, BF16)


def kernel(x, c, ada_w, ada_b, norm_mix_g, norm_mlp_g, w_in, diff_lambda, diff_subln_g, hgrn_lb_logits,
           hgrn_norm_g, gla_gate_w2, gla_gate_b, gla_norm_g, w_up_a, w_up_b, w_up_c, w_out, mlp_w1, mlp_w2,
           final_norm_g):
    b, s, d = x.shape
    tm = min(512, s)
    tq = min(512, s)
    mod = _modulation(c, ada_w, ada_b).reshape(DEPTH, b, N_MOD, 1, d)
    lower_bounds = _lower_bounds(hgrn_lb_logits)
    hgrn_consts = _scan_constants(SCAN_CHUNK, HB_HEADS, HB_HEAD_DIM, HB_HEAD_DIM)
    gla_consts = _scan_constants(SCAN_CHUNK, GC_HEADS, GC_KEY_DIM, GC_VAL_DIM)
    for l in range(DEPTH):
        sh_m, sc_m, gt_m, sh_f, sc_f, gt_f = [mod[l, :, i] for i in range(N_MOD)]
        pa, pb, pf, pc, plr, pg = _pre_project(
            x, sc_m, sh_m, norm_mix_g[l], _input_projection_groups(w_in[l]), PROJ_OUT_DTYPES, tm)
        o_a = _diff_attention(pa, diff_lambda[l], diff_subln_g[l], l, tq)
        o_b = _scan_call(
            _hgrn_kernel, "hgrn2_scan", [pb, pf],
            [lower_bounds[l], jnp.tile(hgrn_norm_g[l], HB_HEADS).reshape(1, HB_WIDTH)],
            hgrn_consts, b, s, HB_WIDTH, HB_WIDTH)
        w2 = jnp.zeros((LANES, 2 * GC_KEY_WIDTH), F32)
        w2 = w2.at[0:GC_GATE_RANK, 0:GC_KEY_WIDTH].set(gla_gate_w2[l, 0])
        w2 = w2.at[GC_GATE_RANK:2 * GC_GATE_RANK, GC_KEY_WIDTH:].set(gla_gate_w2[l, 1])
        w2_hi = w2.astype(BF16)
        w2_lo = (w2 - w2_hi.astype(F32)).astype(BF16)
        o_c = _scan_call(
            _gla_kernel, "gla_scan", [pc, plr],
            [w2_hi, w2_lo, gla_gate_b[l].reshape(1, 2 * GC_KEY_WIDTH),
             jnp.tile(gla_norm_g[l], GC_HEADS).reshape(1, GC_WIDTH)],
            gla_consts, b, s, GC_KEY_WIDTH, GC_WIDTH)
        weights = [w.astype(BF16) for w in (w_up_a[l], w_up_b[l], w_up_c[l], w_out[l], mlp_w1[l], mlp_w2[l])]
        x = _post_mix_mlp(x, o_a, o_b, o_c, pg, gt_m, sc_f, sh_f, gt_f, norm_mlp_g[l], weights,
                          final_norm_g, l == DEPTH - 1, tm)
    return x
```

```python
import functools
import math

import numpy as np
import jax
import jax.numpy as jnp
from jax import lax
from jax.experimental import pallas as pl
from jax.experimental.pallas import tpu as pltpu

F32 = jnp.float32
BF16 = jnp.bfloat16

D_MODEL = 1024
DEPTH = 4
DA_HEADS = 4
DA_HEAD_DIM = 64
DA_V_DIM = 2 * DA_HEAD_DIM
DA_QK_WIDTH = DA_HEADS * 2 * DA_HEAD_DIM
DA_WIDTH = DA_HEADS * DA_V_DIM
HB_HEADS = 4
HB_HEAD_DIM = 64
HB_WIDTH = HB_HEADS * HB_HEAD_DIM
MAX_NEG_LOGIT = 80.0
GC_HEADS = 4
GC_KEY_DIM = 32
GC_VAL_DIM = 64
GC_KEY_WIDTH = GC_HEADS * GC_KEY_DIM
GC_WIDTH = GC_HEADS * GC_VAL_DIM
GC_GATE_RANK = 16
GC_GATE_NORMALIZER = 16.0
N_BRANCH = 3
D_FF = 4 * D_MODEL
N_MOD = 6
EPS = 1e-6
LOG2_E = 1.4426950408889634
ATTN_SUB_ROWS = 256

LANES = 128
SCAN_CHUNK = 64
SCAN_CHUNKS_PER_STEP = 4
V7X_VMEM_BYTES = 64 * 1024 * 1024
VMEM_LIMIT = V7X_VMEM_BYTES - 8 * 1024 * 1024


def _dot(a, b):
    return jnp.dot(a, b, preferred_element_type=F32)


def _dot_nt(a, b):
    return lax.dot_general(a, b, (((1,), (1,)), ((), ())), preferred_element_type=F32)


def _dot_tn(a, b):
    return lax.dot_general(a, b, (((0,), (0,)), ((), ())), preferred_element_type=F32)


def _split_bf16(a):
    hi = a.astype(BF16)
    lo = (a - hi.astype(F32)).astype(BF16)
    return hi, lo


def _sigmoid(a):
    return 1.0 / (1.0 + jnp.exp(-a))


def _silu(a):
    return a * _sigmoid(a)


def _log_sigmoid(z):
    return -(jnp.maximum(-z, 0.0) + jnp.log(1.0 + jnp.exp(-jnp.abs(z))))


def _params(*sem):
    return pltpu.CompilerParams(dimension_semantics=sem, vmem_limit_bytes=VMEM_LIMIT)


def _const_spec(shape):
    n = len(shape)
    return pl.BlockSpec(shape, lambda *_: (0,) * n, pipeline_mode=pl.Buffered(1))


def _mod_kernel(c_ref, w_ref, b_ref, o_ref):
    cond = _silu(c_ref[...])
    o_ref[0] = jnp.dot(cond, w_ref[0], preferred_element_type=F32,
                       precision=lax.Precision.HIGHEST) + b_ref[0]


def _modulation(c, ada_w, ada_b):
    depth, d, n = ada_w.shape
    b = c.shape[0]
    tn = 1536
    return pl.pallas_call(
        _mod_kernel,
        grid=(depth, n // tn),
        in_specs=[
            pl.BlockSpec((b, d), lambda l, j: (0, 0)),
            pl.BlockSpec((1, d, tn), lambda l, j: (l, 0, j)),
            pl.BlockSpec((1, 1, tn), lambda l, j: (l, 0, j)),
        ],
        out_specs=pl.BlockSpec((1, b, tn), lambda l, j: (l, 0, j)),
        out_shape=jax.ShapeDtypeStruct((depth, b, n), F32),
        compiler_params=_params("arbitrary", "arbitrary"),
        name="modulation",
    )(c, ada_w, ada_b.reshape(depth, 1, n))


def _lb_kernel(z_ref, o_ref):
    z = z_ref[...]
    e = jnp.exp(z - jnp.max(z, axis=0, keepdims=True))
    w = e / jnp.sum(e, axis=0, keepdims=True)
    run = jnp.zeros_like(w[0:1])
    for l in range(z.shape[0]):
        run = run + w[l:l + 1]
        o_ref[l:l + 1, :] = run - w[0:1]


def _lower_bounds(logits):
    depth = logits.shape[0]
    flat = logits.reshape(depth, -1).astype(F32)
    out = pl.pallas_call(
        _lb_kernel,
        out_shape=jax.ShapeDtypeStruct(flat.shape, F32),
        name="hgrn_lower_bounds",
    )(flat)
    return out.reshape(logits.shape)


PROJ_COL_CHUNK = 512


def _pre_kernel(x_ref, sc_ref, sh_ref, g_ref, *refs):
    n = len(refs) // 2
    w_refs, o_refs = refs[:n], refs[n:]
    x = x_ref[0]
    ms = jnp.mean(x * x, axis=-1, keepdims=True)
    y = x * lax.rsqrt(ms + EPS) * g_ref[...]
    h = (y * (1.0 + sc_ref[0]) + sh_ref[0]).astype(BF16)
    for w_ref, o_ref in zip(w_refs, o_refs):
        width = w_ref.shape[1]
        for c0 in range(0, width, PROJ_COL_CHUNK):
            c1 = min(c0 + PROJ_COL_CHUNK, width)
            o_ref[0, :, c0:c1] = _dot(h, w_ref[:, c0:c1]).astype(o_ref.dtype)


def _pre_project(x, sc, sh, g, weights, out_dtypes, tm):
    b, s, d = x.shape
    in_specs = [
        pl.BlockSpec((1, tm, d), lambda i, j: (i, j, 0)),
        pl.BlockSpec((1, 1, d), lambda i, j: (i, 0, 0)),
        pl.BlockSpec((1, 1, d), lambda i, j: (i, 0, 0)),
        _const_spec((1, d)),
    ] + [_const_spec(w.shape) for w in weights]
    out_specs = [pl.BlockSpec((1, tm, w.shape[1]), lambda i, j: (i, j, 0)) for w in weights]
    out_shape = [jax.ShapeDtypeStruct((b, s, w.shape[1]), dt) for w, dt in zip(weights, out_dtypes)]
    return pl.pallas_call(
        _pre_kernel,
        grid=(b, s // tm),
        in_specs=in_specs,
        out_specs=out_specs,
        out_shape=out_shape,
        compiler_params=_params("parallel", "arbitrary"),
        name="prenorm_project",
    )(x, sc, sh, g.reshape(1, d), *weights)


def _attn_kernel(q_ref, k_ref, v_ref, slope_ref, lam_ref, g_ref, o_ref, bias_ref, *, lam_init, tq):
    qi = pl.program_id(0)

    @pl.when(pl.program_id(2) == 0)
    def _():
        s = bias_ref.shape[1]
        row = lax.broadcasted_iota(jnp.int32, (tq, s), 0) + qi * tq
        col = lax.broadcasted_iota(jnp.int32, (tq, s), 1)
        dist = jnp.abs(row - col).astype(F32)
        bias_ref[...] = -(slope_ref[0][:, 0:1] * LOG2_E) * dist

    first = lax.broadcasted_iota(jnp.int32, (1, 2 * DA_HEAD_DIM), 1) < DA_HEAD_DIM
    k = k_ref[0]
    v = v_ref[0]
    sub = min(ATTN_SUB_ROWS, tq)

    def scores(r0):
        qs = (q_ref[0, r0:r0 + sub, :].astype(F32) * (DA_HEAD_DIM ** -0.5 * LOG2_E)).astype(BF16)
        zero = jnp.zeros_like(qs)
        return _dot_nt(jnp.where(first, qs, zero), k), _dot_nt(jnp.where(first, zero, qs), k)

    v_ones = jnp.concatenate([v, jnp.ones_like(v)], axis=1)

    def softmax_v(sc, r0):
        m = jnp.max(sc + bias_ref[r0:r0 + sub, :], axis=-1, keepdims=True)
        pv = _dot(jnp.exp2((sc - m) + bias_ref[r0:r0 + sub, :]).astype(BF16), v_ones)
        return pv[:, :DA_V_DIM] / pv[:, DA_V_DIM:DA_V_DIM + 1]

    lp = lam_ref[...]
    lam = (jnp.exp(jnp.sum(lp[0:1] * lp[1:2], axis=-1, keepdims=True))
           - jnp.exp(jnp.sum(lp[2:3] * lp[3:4], axis=-1, keepdims=True)) + lam_init)
    nxt = scores(0)
    for r0 in range(0, tq, sub):
        sc1, sc2 = nxt
        if r0 + sub < tq:
            nxt = scores(r0 + sub)
        o = softmax_v(sc1, r0) - lam * softmax_v(sc2, r0)
        y = o * lax.rsqrt(jnp.mean(o * o, axis=-1, keepdims=True) + EPS) * g_ref[...]
        o_ref[0, r0:r0 + sub, :] = (y * (1.0 - lam_init)).astype(o_ref.dtype)


def _diff_attention(pa, lam_params, subln_g, layer_idx, tq):
    b, s, _ = pa.shape
    h = DA_HEADS
    w = DA_V_DIM
    lam_init = 0.8 - 0.6 * math.exp(-0.3 * layer_idx)
    slopes = np.array([2.0 ** (-8.0 * (i + 1) / h) for i in range(h)], np.float32)
    slopes = jnp.asarray(np.broadcast_to(slopes[:, None, None], (h, 1, LANES)).copy())
    return pl.pallas_call(
        functools.partial(_attn_kernel, lam_init=lam_init, tq=tq),
        grid=(s // tq, h, b),
        in_specs=[
            pl.BlockSpec((1, tq, w), lambda qi, hh, bb: (bb, qi, hh)),
            pl.BlockSpec((1, s, w), lambda qi, hh, bb: (bb, 0, h + hh)),
            pl.BlockSpec((1, s, w), lambda qi, hh, bb: (bb, 0, 2 * h + hh)),
            pl.BlockSpec((1, 1, LANES), lambda qi, hh, bb: (hh, 0, 0)),
            pl.BlockSpec(lam_params.shape, lambda qi, hh, bb: (0, 0)),
            pl.BlockSpec((1, w), lambda qi, hh, bb: (0, 0)),
        ],
        out_specs=pl.BlockSpec((1, tq, w), lambda qi, hh, bb: (bb, qi, hh)),
        out_shape=jax.ShapeDtypeStruct((b, s, h * w), BF16),
        scratch_shapes=[pltpu.VMEM((tq, s), F32)],
        compiler_params=_params("arbitrary", "arbitrary", "arbitrary"),
        name="diff_attention",
    )(pa, pa, pa, slopes, lam_params.astype(F32), subln_g.reshape(1, w).astype(F32))


def _scan_constants(c, h, dk, dv):
    n_lev = int(math.log2(c))
    assert 2 ** n_lev == c
    t = np.arange(c)[:, None]
    r = np.arange(c)[None, :]
    blocks = [(r <= t), (r > t)]
    lev = np.full((c, c), -1, np.int32)
    lev[np.arange(c), np.arange(c)] = n_lev
    for li in range(n_lev):
        bs = c >> (li + 1)
        tb, rb = t // bs, r // bs
        blocks.append((rb == tb) & np.where(tb % 2 == 1, r <= t, r > t))
        lev[(tb % 2 == 1) & (rb == tb - 1)] = li
    blocks.append(np.ones((8, c), bool))
    flip = lambda m: m[::-1, ::-1]
    twice = lambda ms: np.tile(np.concatenate(ms, axis=0).astype(np.float32), (1, 2))
    w_f = twice(blocks)
    w_b = twice([flip(m) for m in blocks])
    head_of_row = np.repeat(np.arange(h), c)[:, None]
    kmask = (head_of_row == (np.arange(h * dk) // dk)[None, :]).astype(np.float32).T
    vmask = (head_of_row == (np.arange(h * dv) // dv)[None, :]).astype(np.float32)
    bd = ((np.arange(h * dv) // dv)[:, None] == (np.arange(h * dk) // dk)[None, :]).astype(np.float32)
    bdn = ((np.arange(h * dv) // dv)[:, None] == (np.arange(h * dv) // dv)[None, :]).astype(np.float32)
    return dict(
        w_f=jnp.asarray(w_f, BF16), w_b=jnp.asarray(w_b, BF16),
        lev_f=jnp.asarray(np.tile(lev, (1, h))), lev_b=jnp.asarray(np.tile(flip(lev), (1, h))),
        kmask=jnp.asarray(kmask, BF16), vmask=jnp.asarray(vmask, BF16),
        bd=jnp.asarray(bd), bdn=jnp.asarray(bdn, BF16),
    )


SCAN_CONST_NAMES = ("w_f", "w_b", "lev_f", "lev_b", "kmask", "vmask", "bd", "bdn")


def _scan_chunks(chains, kmask_ref, vmask_ref, bd_ref, n_heads):
    c = chains[0][0].shape[0]
    n_lev = int(math.log2(c))
    kmask = kmask_ref[...]
    vmask = vmask_ref[...]
    blk = lambda ex, i: ex[i * c:(i + 1) * c]

    def scores(qa, ka):
        kt = jnp.concatenate([ka, ka], axis=0).T
        return _dot(qa, jnp.concatenate([kt] * (n_heads // 2), axis=1) * kmask)

    exs = [None] * len(chains)
    for first in range(len(chains)):
        w_ref = chains[first][5]
        if exs[first] is not None:
            continue
        idx = [i for i, ch in enumerate(chains) if ch[5] is w_ref]
        parts = [_split_bf16(chains[i][3]) for i in idx]
        rhs = jnp.concatenate([jnp.concatenate([hi, lo], axis=0) for hi, lo in parts], axis=1)
        ex = jnp.exp2(_dot(w_ref[...], rhs))
        width = chains[idx[0]][3].shape[1]
        for n, i in enumerate(idx):
            exs[i] = ex[:, n * width:(n + 1) * width]
    accs = [jnp.where(ch[6][...] == n_lev, scores(ch[0].astype(BF16), ch[1].astype(BF16)), 0.0)
            for ch in chains]
    for li in range(n_lev):
        for i, (ch, ex) in enumerate(zip(chains, exs)):
            e = blk(ex, 2 + li)
            r = scores((ch[0] * e).astype(BF16), (ch[1] * e).astype(BF16))
            accs[i] = jnp.where(ch[6][...] == li, r, accs[i])
    outs = []
    for ch, acc in zip(chains, accs):
        v_stack = jnp.concatenate([ch[2]] * n_heads, axis=0) * vmask
        outs.append(_dot(acc.astype(BF16), v_stack))
    news = [_dot_tn(ch[2], (ch[1] * blk(ex, 1)).astype(BF16)) for ch, ex in zip(chains, exs)]
    for i, (ch, ex) in enumerate(zip(chains, exs)):
        st_ref = ch[4]
        st = st_ref[...]
        outs[i] = outs[i] + _dot_nt((ch[0] * blk(ex, 0)).astype(BF16), st.astype(BF16))
        e_tot = ex[(2 + n_lev) * c:(2 + n_lev) * c + 1]
        st_ref[...] = st * e_tot + news[i] * bd_ref[...]
    return outs


def _head_rmsnorm_gate(o, gain, gate, bdn, dv):
    hi, lo = _split_bf16(o * o)
    ms = (_dot(hi, bdn) + _dot(lo, bdn)) * (1.0 / dv)
    return o * lax.rsqrt(ms + EPS) * gain * _silu(gate)


EPILOGUE_ROWS = 256


PROLOGUE_ROWS = 256


def _bidir_scan(prologue, prep_f, prep_b, gate_fn, gain_ref, consts, o_ref, of_ref, ob_ref, stf_ref,
                stb_ref, n_heads, dv):
    w_f, w_b, lev_f, lev_b, kmask, vmask, bd, bdn = consts
    s = of_ref.shape[0]
    c = SCAN_CHUNK
    nc = s // c
    group = SCAN_CHUNKS_PER_STEP
    stf_ref[...] = jnp.zeros_like(stf_ref)
    stb_ref[...] = jnp.zeros_like(stb_ref)
    assert nc % group == 0
    pro_rows = min(PROLOGUE_ROWS, s)

    def pro_body(j, carry):
        prologue(pl.multiple_of(j * pro_rows, pro_rows), pro_rows)
        return carry

    lax.fori_loop(0, s // pro_rows, pro_body, 0, unroll=2)

    def body(j, carry):
        rows, chains = [], []
        for g in range(group):
            rf = pl.multiple_of((j * group + g) * c, c)
            rb = pl.multiple_of((nc - 1 - j * group - g) * c, c)
            rows += [(of_ref, rf), (ob_ref, rb)]
            chains += [prep_f(rf) + (stf_ref, w_f, lev_f), prep_b(rb) + (stb_ref, w_b, lev_b)]
        outs = _scan_chunks(chains, kmask, vmask, bd, n_heads)
        for (ref, r0), o in zip(rows, outs):
            ref[pl.ds(r0, c), :] = o
        return carry

    lax.fori_loop(0, nc // group, body, 0)
    rows = min(EPILOGUE_ROWS, s)
    for r0 in range(0, s, rows):
        o = of_ref[r0:r0 + rows, :] + ob_ref[r0:r0 + rows, :]
        y = _head_rmsnorm_gate(o, gain_ref[...], gate_fn(r0, rows), bdn[...], dv)
        o_ref[0, r0:r0 + rows, :] = y.astype(o_ref.dtype)


def _hgrn_kernel(pb_ref, pf_ref, lb_ref, gain_ref, *refs):
    consts = refs[:8]
    o_ref, of_ref, ob_ref, stf_ref, stb_ref, q_ref, kf_ref, kb_ref, laf_ref, lab_ref = refs[8:]
    k_refs = (kf_ref, kb_ref)
    la_refs = (laf_ref, lab_ref)
    c = SCAN_CHUNK
    wd = HB_WIDTH

    def prologue(r0, rows):
        q_ref[pl.ds(r0, rows), :] = (_silu(pb_ref[0, pl.ds(r0, rows), 0:wd].astype(F32))
                                     * HB_HEAD_DIM ** -0.5)
        for direction in range(2):
            lb = lb_ref[direction:direction + 1, :]
            z = pf_ref[0, pl.ds(r0, rows), direction * wd:(direction + 1) * wd]
            nz2 = z * -LOG2_E
            e = jnp.exp2(-jnp.abs(nz2))
            r = 1.0 / (1.0 + e)
            cap = jnp.exp2(jnp.minimum(nz2, MAX_NEG_LOGIT * LOG2_E))
            la_refs[direction][pl.ds(r0, rows), :] = (
                jnp.log((1.0 + lb * cap) * r) * LOG2_E - jnp.maximum(nz2, 0.0))
            k_refs[direction][pl.ds(r0, rows), :] = (1.0 - lb) * jnp.where(z >= 0.0, e * r, r)

    def prep(direction):
        def fn(r0):
            rows = pl.ds(r0, c)
            return (q_ref[rows, :], k_refs[direction][rows, :], pb_ref[0, rows, wd:2 * wd],
                    la_refs[direction][rows, :])
        return fn

    gate_fn = lambda r0, rows: pb_ref[0, r0:r0 + rows, 2 * wd:3 * wd].astype(F32)
    _bidir_scan(prologue, prep(0), prep(1), gate_fn, gain_ref, consts, o_ref, of_ref, ob_ref, stf_ref,
                stb_ref, HB_HEADS, HB_HEAD_DIM)


def _gla_kernel(pc_ref, plr_ref, w2h_ref, w2l_ref, gb_ref, gain_ref, *refs):
    consts = refs[:8]
    o_ref, of_ref, ob_ref, stf_ref, stb_ref, la_ref = refs[8:]
    c = SCAN_CHUNK
    kw = GC_KEY_WIDTH
    vw = GC_WIDTH

    def prologue(r0, rows):
        lr_hi, lr_lo = _split_bf16(plr_ref[0, pl.ds(r0, rows), :])
        w2h = w2h_ref[...]
        z = _dot(lr_hi, w2h) + _dot(lr_lo, w2h) + _dot(lr_hi, w2l_ref[...]) + gb_ref[...]
        la_ref[pl.ds(r0, rows), :] = _log_sigmoid(z) * (LOG2_E / GC_GATE_NORMALIZER)

    def prep(direction):
        def fn(r0):
            rows = pl.ds(r0, c)
            qkv = pc_ref[0, rows, :]
            q = qkv[:, 0:kw].astype(F32) * GC_KEY_DIM ** -0.5
            k = qkv[:, kw:2 * kw].astype(F32)
            v = qkv[:, 2 * kw:2 * kw + vw]
            return q, k, v, la_ref[rows, direction * kw:(direction + 1) * kw]
        return fn

    gate_fn = lambda r0, rows: pc_ref[0, r0:r0 + rows, 2 * kw + vw:2 * kw + 2 * vw].astype(F32)
    _bidir_scan(prologue, prep(0), prep(1), gate_fn, gain_ref, consts, o_ref, of_ref, ob_ref, stf_ref,
                stb_ref, GC_HEADS, GC_VAL_DIM)


def _scan_call(kernel, name, seq_inputs, small_inputs, consts, b, s, hk, hv, token_scratch_widths):
    const_arrays = [consts[n] for n in SCAN_CONST_NAMES]
    in_specs = ([pl.BlockSpec((1, s, a.shape[2]), lambda i: (i, 0, 0)) for a in seq_inputs]
                + [_const_spec(a.shape) for a in small_inputs]
                + [_const_spec(a.shape) for a in const_arrays])
    return pl.pallas_call(
        kernel,
        grid=(b,),
        in_specs=in_specs,
        out_specs=pl.BlockSpec((1, s, hv), lambda i: (i, 0, 0)),
        out_shape=jax.ShapeDtypeStruct((b, s, hv), BF16),
        scratch_shapes=[pltpu.VMEM((s, hv), F32), pltpu.VMEM((s, hv), F32),
                        pltpu.VMEM((hv, hk), F32), pltpu.VMEM((hv, hk), F32)]
        + [pltpu.VMEM((s, w), F32) for w in token_scratch_widths],
        compiler_params=_params("parallel"),
        name=name,
    )(*seq_inputs, *small_inputs, *const_arrays)


FF_CHUNK = 1024


def _post_kernel(x_ref, oa_ref, ob_ref, oc_ref, pg_ref, gtm_ref, scf_ref, shf_ref, gtf_ref, g2_ref,
                 wua_ref, wub_ref, wuc_ref, wo_ref, w1_ref, w2_ref, fg_ref, out_ref, *, final):
    d = x_ref.shape[2]
    gate = lambda i: _sigmoid(pg_ref[0, :, i * d:(i + 1) * d].astype(F32))
    merged = (gate(0) * _dot(oa_ref[0], wua_ref[...])
              + gate(1) * _dot(ob_ref[0], wub_ref[...])
              + gate(2) * _dot(oc_ref[0], wuc_ref[...]))
    x1 = x_ref[0] + gtm_ref[0] * _dot(merged.astype(BF16), wo_ref[...])
    ms = jnp.mean(x1 * x1, axis=-1, keepdims=True)
    y = x1 * lax.rsqrt(ms + EPS) * g2_ref[...]
    h = (y * (1.0 + scf_ref[0]) + shf_ref[0]).astype(BF16)
    acc = jnp.zeros_like(x1)
    for c0 in range(0, w1_ref.shape[1], FF_CHUNK):
        u = jnp.maximum(_dot(h, w1_ref[:, c0:c0 + FF_CHUNK]), 0.0)
        acc = acc + _dot((u * u).astype(BF16), w2_ref[c0:c0 + FF_CHUNK, :])
    x2 = x1 + gtf_ref[0] * acc
    if final:
        x2 = x2 * lax.rsqrt(jnp.mean(x2 * x2, axis=-1, keepdims=True) + EPS) * fg_ref[...]
    out_ref[0] = x2


def _post_mix_mlp(x, oa, ob, oc, pg, gtm, scf, shf, gtf, g2, weights, fg, final, tm):
    b, s, d = x.shape
    tok = lambda a: pl.BlockSpec((1, tm, a.shape[2]), lambda i, j: (i, j, 0))
    vec = pl.BlockSpec((1, 1, d), lambda i, j: (i, 0, 0))
    return pl.pallas_call(
        functools.partial(_post_kernel, final=final),
        grid=(b, s // tm),
        in_specs=[tok(x), tok(oa), tok(ob), tok(oc), tok(pg), vec, vec, vec, vec, _const_spec((1, d))]
        + [_const_spec(w.shape) for w in weights] + [_const_spec((1, d))],
        out_specs=pl.BlockSpec((1, tm, d), lambda i, j: (i, j, 0)),
        out_shape=jax.ShapeDtypeStruct((b, s, d), F32),
        compiler_params=_params("parallel", "arbitrary"),
        name="merge_out_mlp",
    )(x, oa, ob, oc, pg, gtm, scf, shf, gtf, g2.reshape(1, d), *weights, fg.reshape(1, d))


def _input_projection_groups(w):
    o_b = 2 * DA_QK_WIDTH + DA_WIDTH
    o_c = o_b + 5 * HB_WIDTH
    o_lr = o_c + 2 * GC_KEY_WIDTH + 2 * GC_WIDTH
    o_g = o_lr + 2 * GC_GATE_RANK
    wb = HB_WIDTH
    attn = w[:, :o_b]
    hgrn_qig = jnp.concatenate([w[:, o_b:o_b + wb], w[:, o_b + 3 * wb:o_b + 5 * wb]], axis=1)
    hgrn_f = w[:, o_b + wb:o_b + 3 * wb]
    gla = w[:, o_c:o_lr]
    lr = jnp.pad(w[:, o_lr:o_g], ((0, 0), (0, LANES - 2 * GC_GATE_RANK)))
    gates = w[:, o_g:]
    return [g.astype(BF16) for g in (attn, hgrn_qig, hgrn_f, gla, lr, gates)]


PROJ_OUT_DTYPES = (BF16, BF16, F32, BF16, F32, BF16)


def kernel(x, c, ada_w, ada_b, norm_mix_g, norm_mlp_g, w_in, diff_lambda, diff_subln_g, hgrn_lb_logits,
           hgrn_norm_g, gla_gate_w2, gla_gate_b, gla_norm_g, w_up_a, w_up_b, w_up_c, w_out, mlp_w1, mlp_w2,
           final_norm_g):
    b, s, d = x.shape
    tm = min(512, s)
    tq = min(512, s)
    mod = _modulation(c, ada_w, ada_b).reshape(DEPTH, b, N_MOD, 1, d)
    lower_bounds = _lower_bounds(hgrn_lb_logits)
    hgrn_consts = _scan_constants(SCAN_CHUNK, HB_HEADS, HB_HEAD_DIM, HB_HEAD_DIM)
    gla_consts = _scan_constants(SCAN_CHUNK, GC_HEADS, GC_KEY_DIM, GC_VAL_DIM)
    for l in range(DEPTH):
        sh_m, sc_m, gt_m, sh_f, sc_f, gt_f = [mod[l, :, i] for i in range(N_MOD)]
        pa, pb, pf, pc, plr, pg = _pre_project(
            x, sc_m, sh_m, norm_mix_g[l], _input_projection_groups(w_in[l]), PROJ_OUT_DTYPES, tm)
        o_a = _diff_attention(pa, diff_lambda[l], diff_subln_g[l], l, tq)
        o_b = _scan_call(
            _hgrn_kernel, "hgrn2_scan", [pb, pf],
            [lower_bounds[l], jnp.tile(hgrn_norm_g[l], HB_HEADS).reshape(1, HB_WIDTH)],
            hgrn_consts, b, s, HB_WIDTH, HB_WIDTH, [HB_WIDTH] * 5)
        w2 = jnp.zeros((LANES, 2 * GC_KEY_WIDTH), F32)
        w2 = w2.at[0:GC_GATE_RANK, 0:GC_KEY_WIDTH].set(gla_gate_w2[l, 0])
        w2 = w2.at[GC_GATE_RANK:2 * GC_GATE_RANK, GC_KEY_WIDTH:].set(gla_gate_w2[l, 1])
        w2_hi = w2.astype(BF16)
        w2_lo = (w2 - w2_hi.astype(F32)).astype(BF16)
        o_c = _scan_call(
            _gla_kernel, "gla_scan", [pc, plr],
            [w2_hi, w2_lo, gla_gate_b[l].reshape(1, 2 * GC_KEY_WIDTH),
             jnp.tile(gla_norm_g[l], GC_HEADS).reshape(1, GC_WIDTH)],
            gla_consts, b, s, GC_KEY_WIDTH, GC_WIDTH, [2 * GC_KEY_WIDTH])
        weights = [w.astype(BF16) for w in (w_up_a[l], w_up_b[l], w_up_c[l], w_out[l], mlp_w1[l], mlp_w2[l])]
        x = _post_mix_mlp(x, o_a, o_b, o_c, pg, gt_m, sc_f, sh_f, gt_f, norm_mlp_g[l], weights,
                          final_norm_g, l == DEPTH - 1, tm)
    return x
```

```python
import functools
import math

import numpy as np
import jax
import jax.numpy as jnp
from jax import lax
from jax.experimental import pallas as pl
from jax.experimental.pallas import tpu as pltpu

F32 = jnp.float32
BF16 = jnp.bfloat16

D_MODEL = 1024
DEPTH = 4
DA_HEADS = 4
DA_HEAD_DIM = 64
DA_V_DIM = 2 * DA_HEAD_DIM
DA_QK_WIDTH = DA_HEADS * 2 * DA_HEAD_DIM
DA_WIDTH = DA_HEADS * DA_V_DIM
HB_HEADS = 4
HB_HEAD_DIM = 64
HB_WIDTH = HB_HEADS * HB_HEAD_DIM
MAX_NEG_LOGIT = 80.0
GC_HEADS = 4
GC_KEY_DIM = 32
GC_VAL_DIM = 64
GC_KEY_WIDTH = GC_HEADS * GC_KEY_DIM
GC_WIDTH = GC_HEADS * GC_VAL_DIM
GC_GATE_RANK = 16
GC_GATE_NORMALIZER = 16.0
N_BRANCH = 3
D_FF = 4 * D_MODEL
N_MOD = 6
EPS = 1e-6
LOG2_E = 1.4426950408889634
ATTN_SUB_ROWS = 256

LANES = 128
SCAN_CHUNK = 64
SCAN_CHUNKS_PER_STEP = 4
V7X_VMEM_BYTES = 64 * 1024 * 1024
VMEM_LIMIT = V7X_VMEM_BYTES - 8 * 1024 * 1024


def _dot(a, b):
    return jnp.dot(a, b, preferred_element_type=F32)


def _dot_nt(a, b):
    return lax.dot_general(a, b, (((1,), (1,)), ((), ())), preferred_element_type=F32)


def _dot_tn(a, b):
    return lax.dot_general(a, b, (((0,), (0,)), ((), ())), preferred_element_type=F32)


def _split_bf16(a):
    hi = a.astype(BF16)
    lo = (a - hi.astype(F32)).astype(BF16)
    return hi, lo


def _sigmoid(a):
    return 1.0 / (1.0 + jnp.exp(-a))


def _silu(a):
    return a * _sigmoid(a)


def _log_sigmoid(z):
    return -(jnp.maximum(-z, 0.0) + jnp.log(1.0 + jnp.exp(-jnp.abs(z))))


def _params(*sem):
    return pltpu.CompilerParams(dimension_semantics=sem, vmem_limit_bytes=VMEM_LIMIT)


def _const_spec(shape):
    n = len(shape)
    return pl.BlockSpec(shape, lambda *_: (0,) * n, pipeline_mode=pl.Buffered(1))


def _mod_kernel(c_ref, w_ref, b_ref, o_ref):
    cond = _silu(c_ref[...])
    o_ref[0] = jnp.dot(cond, w_ref[0], preferred_element_type=F32,
                       precision=lax.Precision.HIGHEST) + b_ref[0]


def _modulation(c, ada_w, ada_b):
    depth, d, n = ada_w.shape
    b = c.shape[0]
    tn = 1536
    return pl.pallas_call(
        _mod_kernel,
        grid=(depth, n // tn),
        in_specs=[
            pl.BlockSpec((b, d), lambda l, j: (0, 0)),
            pl.BlockSpec((1, d, tn), lambda l, j: (l, 0, j)),
            pl.BlockSpec((1, 1, tn), lambda l, j: (l, 0, j)),
        ],
        out_specs=pl.BlockSpec((1, b, tn), lambda l, j: (l, 0, j)),
        out_shape=jax.ShapeDtypeStruct((depth, b, n), F32),
        compiler_params=_params("arbitrary", "arbitrary"),
        name="modulation",
    )(c, ada_w, ada_b.reshape(depth, 1, n))


def _lb_kernel(z_ref, o_ref):
    z = z_ref[...]
    e = jnp.exp(z - jnp.max(z, axis=0, keepdims=True))
    w = e / jnp.sum(e, axis=0, keepdims=True)
    run = jnp.zeros_like(w[0:1])
    for l in range(z.shape[0]):
        run = run + w[l:l + 1]
        o_ref[l:l + 1, :] = run - w[0:1]


def _lower_bounds(logits):
    depth = logits.shape[0]
    flat = logits.reshape(depth, -1).astype(F32)
    out = pl.pallas_call(
        _lb_kernel,
        out_shape=jax.ShapeDtypeStruct(flat.shape, F32),
        name="hgrn_lower_bounds",
    )(flat)
    return out.reshape(logits.shape)


PROJ_COL_CHUNK = 512


def _pre_kernel(x_ref, sc_ref, sh_ref, g_ref, *refs):
    n = len(refs) // 2
    w_refs, o_refs = refs[:n], refs[n:]
    x = x_ref[0]
    ms = jnp.mean(x * x, axis=-1, keepdims=True)
    y = x * lax.rsqrt(ms + EPS) * g_ref[...]
    h = (y * (1.0 + sc_ref[0]) + sh_ref[0]).astype(BF16)
    for w_ref, o_ref in zip(w_refs, o_refs):
        width = w_ref.shape[1]
        for c0 in range(0, width, PROJ_COL_CHUNK):
            c1 = min(c0 + PROJ_COL_CHUNK, width)
            o_ref[0, :, c0:c1] = _dot(h, w_ref[:, c0:c1]).astype(o_ref.dtype)


def _pre_project(x, sc, sh, g, weights, out_dtypes, tm):
    b, s, d = x.shape
    in_specs = [
        pl.BlockSpec((1, tm, d), lambda i, j: (i, j, 0)),
        pl.BlockSpec((1, 1, d), lambda i, j: (i, 0, 0)),
        pl.BlockSpec((1, 1, d), lambda i, j: (i, 0, 0)),
        _const_spec((1, d)),
    ] + [_const_spec(w.shape) for w in weights]
    out_specs = [pl.BlockSpec((1, tm, w.shape[1]), lambda i, j: (i, j, 0)) for w in weights]
    out_shape = [jax.ShapeDtypeStruct((b, s, w.shape[1]), dt) for w, dt in zip(weights, out_dtypes)]
    return pl.pallas_call(
        _pre_kernel,
        grid=(b, s // tm),
        in_specs=in_specs,
        out_specs=out_specs,
        out_shape=out_shape,
        compiler_params=_params("parallel", "arbitrary"),
        name="prenorm_project",
    )(x, sc, sh, g.reshape(1, d), *weights)


def _attn_kernel(q_ref, k_ref, v_ref, slope_ref, lam_ref, g_ref, o_ref, bias_ref, *, lam_init, tq):
    qi = pl.program_id(0)

    @pl.when(pl.program_id(2) == 0)
    def _():
        s = bias_ref.shape[1]
        row = lax.broadcasted_iota(jnp.int32, (tq, s), 0) + qi * tq
        col = lax.broadcasted_iota(jnp.int32, (tq, s), 1)
        dist = jnp.abs(row - col).astype(F32)
        bias_ref[...] = -(slope_ref[0][:, 0:1] * LOG2_E) * dist

    first = lax.broadcasted_iota(jnp.int32, (1, 2 * DA_HEAD_DIM), 1) < DA_HEAD_DIM
    k = k_ref[0]
    v = v_ref[0]
    sub = min(ATTN_SUB_ROWS, tq)

    def scores(r0):
        qs = (q_ref[0, r0:r0 + sub, :].astype(F32) * (DA_HEAD_DIM ** -0.5 * LOG2_E)).astype(BF16)
        zero = jnp.zeros_like(qs)
        return _dot_nt(jnp.where(first, qs, zero), k), _dot_nt(jnp.where(first, zero, qs), k)

    v_ones = jnp.concatenate([v, jnp.ones_like(v)], axis=1)

    def softmax_v(sc, r0):
        m = jnp.max(sc + bias_ref[r0:r0 + sub, :], axis=-1, keepdims=True)
        pv = _dot(jnp.exp2((sc - m) + bias_ref[r0:r0 + sub, :]).astype(BF16), v_ones)
        return pv[:, :DA_V_DIM] / pv[:, DA_V_DIM:DA_V_DIM + 1]

    lp = lam_ref[...]
    lam = (jnp.exp(jnp.sum(lp[0:1] * lp[1:2], axis=-1, keepdims=True))
           - jnp.exp(jnp.sum(lp[2:3] * lp[3:4], axis=-1, keepdims=True)) + lam_init)
    nxt = scores(0)
    for r0 in range(0, tq, sub):
        sc1, sc2 = nxt
        if r0 + sub < tq:
            nxt = scores(r0 + sub)
        o = softmax_v(sc1, r0) - lam * softmax_v(sc2, r0)
        y = o * lax.rsqrt(jnp.mean(o * o, axis=-1, keepdims=True) + EPS) * g_ref[...]
        o_ref[0, r0:r0 + sub, :] = (y * (1.0 - lam_init)).astype(o_ref.dtype)


def _diff_attention(pa, lam_params, subln_g, layer_idx, tq):
    b, s, _ = pa.shape
    h = DA_HEADS
    w = DA_V_DIM
    lam_init = 0.8 - 0.6 * math.exp(-0.3 * layer_idx)
    slopes = np.array([2.0 ** (-8.0 * (i + 1) / h) for i in range(h)], np.float32)
    slopes = jnp.asarray(np.broadcast_to(slopes[:, None, None], (h, 1, LANES)).copy())
    return pl.pallas_call(
        functools.partial(_attn_kernel, lam_init=lam_init, tq=tq),
        grid=(s // tq, h, b),
        in_specs=[
            pl.BlockSpec((1, tq, w), lambda qi, hh, bb: (bb, qi, hh)),
            pl.BlockSpec((1, s, w), lambda qi, hh, bb: (bb, 0, h + hh)),
            pl.BlockSpec((1, s, w), lambda qi, hh, bb: (bb, 0, 2 * h + hh)),
            pl.BlockSpec((1, 1, LANES), lambda qi, hh, bb: (hh, 0, 0)),
            pl.BlockSpec(lam_params.shape, lambda qi, hh, bb: (0, 0)),
            pl.BlockSpec((1, w), lambda qi, hh, bb: (0, 0)),
        ],
        out_specs=pl.BlockSpec((1, tq, w), lambda qi, hh, bb: (bb, qi, hh)),
        out_shape=jax.ShapeDtypeStruct((b, s, h * w), BF16),
        scratch_shapes=[pltpu.VMEM((tq, s), F32)],
        compiler_params=_params("arbitrary", "arbitrary", "arbitrary"),
        name="diff_attention",
    )(pa, pa, pa, slopes, lam_params.astype(F32), subln_g.reshape(1, w).astype(F32))


def _scan_constants(c, h, dk, dv):
    n_lev = int(math.log2(c))
    assert 2 ** n_lev == c
    assert dv & (dv - 1) == 0, "1/dv must be exact in bf16"
    t = np.arange(c)[:, None]
    r = np.arange(c)[None, :]
    blocks = [(r <= t), (r > t)]
    lev = np.full((c, c), -1, np.int32)
    lev[np.arange(c), np.arange(c)] = n_lev
    for li in range(n_lev):
        bs = c >> (li + 1)
        tb, rb = t // bs, r // bs
        blocks.append((rb == tb) & np.where(tb % 2 == 1, r <= t, r > t))
        lev[(tb % 2 == 1) & (rb == tb - 1)] = li
    blocks.append(np.ones((8, c), bool))
    flip = lambda m: m[::-1, ::-1]
    twice = lambda ms: np.tile(np.concatenate(ms, axis=0).astype(np.float32), (1, 2))
    w_f = twice(blocks)
    w_b = twice([flip(m) for m in blocks])
    head_of_row = np.repeat(np.arange(h), c)[:, None]
    kmask = (head_of_row == (np.arange(h * dk) // dk)[None, :]).astype(np.float32).T
    vmask = (head_of_row == (np.arange(h * dv) // dv)[None, :]).astype(np.float32)
    bd = ((np.arange(h * dv) // dv)[:, None] == (np.arange(h * dk) // dk)[None, :]).astype(np.float32)
    bdn = ((np.arange(h * dv) // dv)[:, None] == (np.arange(h * dv) // dv)[None, :]).astype(np.float32) / dv
    return dict(
        w_f=jnp.asarray(w_f, BF16), w_b=jnp.asarray(w_b, BF16),
        lev_f=jnp.asarray(np.tile(lev, (1, h))), lev_b=jnp.asarray(np.tile(flip(lev), (1, h))),
        kmask=jnp.asarray(kmask, BF16), vmask=jnp.asarray(vmask, BF16),
        bd=jnp.asarray(bd), bdn=jnp.asarray(bdn, BF16),
    )


SCAN_CONST_NAMES = ("w_f", "w_b", "lev_f", "lev_b", "kmask", "vmask", "bd", "bdn")


def _scan_chunks(chains, kmask_ref, vmask_ref, bd_ref, n_heads):
    c = chains[0][0].shape[0]
    n_lev = int(math.log2(c))
    kmask = kmask_ref[...]
    vmask = vmask_ref[...]
    blk = lambda ex, i: ex[i * c:(i + 1) * c]

    def scores(qa, ka):
        kt = jnp.concatenate([ka, ka], axis=0).T
        return _dot(qa, jnp.concatenate([kt] * (n_heads // 2), axis=1) * kmask)

    exs = [None] * len(chains)
    for first in range(len(chains)):
        w_ref = chains[first][5]
        if exs[first] is not None:
            continue
        idx = [i for i, ch in enumerate(chains) if ch[5] is w_ref]
        parts = [_split_bf16(chains[i][3]) for i in idx]
        rhs = jnp.concatenate([jnp.concatenate([hi, lo], axis=0) for hi, lo in parts], axis=1)
        ex = jnp.exp2(_dot(w_ref[...], rhs))
        width = chains[idx[0]][3].shape[1]
        for n, i in enumerate(idx):
            exs[i] = ex[:, n * width:(n + 1) * width]
    accs = [jnp.where(ch[6][...] == n_lev, scores(ch[0].astype(BF16), ch[1].astype(BF16)), 0.0)
            for ch in chains]
    for li in range(n_lev):
        for i, (ch, ex) in enumerate(zip(chains, exs)):
            e = blk(ex, 2 + li)
            r = scores((ch[0] * e).astype(BF16), (ch[1] * e).astype(BF16))
            accs[i] = jnp.where(ch[6][...] == li, r, accs[i])
    outs = []
    for ch, acc in zip(chains, accs):
        v_stack = jnp.concatenate([ch[2]] * n_heads, axis=0) * vmask
        outs.append(_dot(acc.astype(BF16), v_stack))
    news = [_dot_tn(ch[2], (ch[1] * blk(ex, 1)).astype(BF16)) for ch, ex in zip(chains, exs)]
    for i, (ch, ex) in enumerate(zip(chains, exs)):
        st_ref = ch[4]
        st = st_ref[...]
        outs[i] = outs[i] + _dot_nt((ch[0] * blk(ex, 0)).astype(BF16), st.astype(BF16))
        e_tot = ex[(2 + n_lev) * c:(2 + n_lev) * c + 1]
        st_ref[...] = st * e_tot + news[i] * bd_ref[...]
    return outs


def _head_rmsnorm_gate(o, gain, gate, bdn):
    ms = _dot((o * o).astype(BF16), bdn)
    return o * lax.rsqrt(ms + EPS) * gain * _silu(gate)


def _bidir_scan(prologue, prep_f, prep_b, gate_fn, gain_ref, consts, o_ref, of_ref, ob_ref, stf_ref,
                stb_ref, n_heads):
    w_f, w_b, lev_f, lev_b, kmask, vmask, bd, bdn = consts
    s = of_ref.shape[0]
    c = SCAN_CHUNK
    group = SCAN_CHUNKS_PER_STEP
    block = group * c
    assert s % block == 0
    nb = s // block
    stf_ref[...] = jnp.zeros_like(stf_ref)
    stb_ref[...] = jnp.zeros_like(stb_ref)

    def prologue_step(jb, carry):
        prologue(pl.multiple_of(jb * block, block), block)
        return carry

    lax.fori_loop(0, nb, prologue_step, 0, unroll=2 if nb % 2 == 0 else 1)

    def step(j, carry):
        rows, chains = [], []
        for g in range(group):
            rf = pl.multiple_of(j * block + g * c, c)
            rb = pl.multiple_of((nb - 1 - j) * block + (group - 1 - g) * c, c)
            rows += [(of_ref, rf), (ob_ref, rb)]
            chains += [prep_f(rf) + (stf_ref, w_f, lev_f), prep_b(rb) + (stb_ref, w_b, lev_b)]
        outs = _scan_chunks(chains, kmask, vmask, bd, n_heads)
        for (ref, r0), o in zip(rows, outs):
            ref[pl.ds(r0, c), :] = o
        return carry

    lax.fori_loop(0, nb, step, 0)
    for jb in range(nb):
        rows = pl.ds(jb * block, block)
        y = _head_rmsnorm_gate(of_ref[rows, :] + ob_ref[rows, :], gain_ref[...],
                               gate_fn(jb * block, block), bdn[...])
        o_ref[0, rows, :] = y.astype(o_ref.dtype)


def _hgrn_kernel(pb_ref, pf_ref, lb_ref, gain_ref, *refs):
    consts = refs[:8]
    o_ref, of_ref, ob_ref, stf_ref, stb_ref, q_ref, kf_ref, kb_ref, laf_ref, lab_ref = refs[8:]
    k_refs = (kf_ref, kb_ref)
    la_refs = (laf_ref, lab_ref)
    c = SCAN_CHUNK
    wd = HB_WIDTH

    def prologue(r0, rows):
        q_ref[pl.ds(r0, rows), :] = (_silu(pb_ref[0, pl.ds(r0, rows), 0:wd].astype(F32))
                                     * HB_HEAD_DIM ** -0.5)
        for direction in range(2):
            lb = lb_ref[direction:direction + 1, :]
            z = pf_ref[0, pl.ds(r0, rows), direction * wd:(direction + 1) * wd]
            nz2 = z * -LOG2_E
            e = jnp.exp2(-jnp.abs(nz2))
            r = 1.0 / (1.0 + e)
            cap = jnp.exp2(jnp.minimum(nz2, MAX_NEG_LOGIT * LOG2_E))
            la_refs[direction][pl.ds(r0, rows), :] = (
                jnp.log((1.0 + lb * cap) * r) * LOG2_E - jnp.maximum(nz2, 0.0))
            k_refs[direction][pl.ds(r0, rows), :] = (1.0 - lb) * jnp.where(z >= 0.0, e * r, r)

    def prep(direction):
        def fn(r0):
            rows = pl.ds(r0, c)
            return (q_ref[rows, :], k_refs[direction][rows, :], pb_ref[0, rows, wd:2 * wd],
                    la_refs[direction][rows, :])
        return fn

    gate_fn = lambda r0, rows: pb_ref[0, pl.ds(r0, rows), 2 * wd:3 * wd].astype(F32)
    _bidir_scan(prologue, prep(0), prep(1), gate_fn, gain_ref, consts, o_ref, of_ref, ob_ref, stf_ref,
                stb_ref, HB_HEADS)


def _gla_kernel(pc_ref, plr_ref, w2h_ref, w2l_ref, gb_ref, gain_ref, *refs):
    consts = refs[:8]
    o_ref, of_ref, ob_ref, stf_ref, stb_ref, la_ref = refs[8:]
    c = SCAN_CHUNK
    kw = GC_KEY_WIDTH
    vw = GC_WIDTH

    def prologue(r0, rows):
        lr_hi, lr_lo = _split_bf16(plr_ref[0, pl.ds(r0, rows), :])
        w2h = w2h_ref[...]
        z = _dot(lr_hi, w2h) + _dot(lr_lo, w2h) + _dot(lr_hi, w2l_ref[...]) + gb_ref[...]
        la_ref[pl.ds(r0, rows), :] = _log_sigmoid(z) * (LOG2_E / GC_GATE_NORMALIZER)

    def prep(direction):
        def fn(r0):
            rows = pl.ds(r0, c)
            qkv = pc_ref[0, rows, :]
            q = qkv[:, 0:kw].astype(F32) * GC_KEY_DIM ** -0.5
            k = qkv[:, kw:2 * kw].astype(F32)
            v = qkv[:, 2 * kw:2 * kw + vw]
            return q, k, v, la_ref[rows, direction * kw:(direction + 1) * kw]
        return fn

    gate_fn = lambda r0, rows: pc_ref[0, pl.ds(r0, rows), 2 * kw + vw:2 * kw + 2 * vw].astype(F32)
    _bidir_scan(prologue, prep(0), prep(1), gate_fn, gain_ref, consts, o_ref, of_ref, ob_ref, stf_ref,
                stb_ref, GC_HEADS)


def _scan_call(kernel, name, seq_inputs, small_inputs, consts, b, s, hk, hv, token_scratch_widths):
    const_arrays = [consts[n] for n in SCAN_CONST_NAMES]
    in_specs = ([pl.BlockSpec((1, s, a.shape[2]), lambda i: (i, 0, 0)) for a in seq_inputs]
                + [_const_spec(a.shape) for a in small_inputs]
                + [_const_spec(a.shape) for a in const_arrays])
    return pl.pallas_call(
        kernel,
        grid=(b,),
        in_specs=in_specs,
        out_specs=pl.BlockSpec((1, s, hv), lambda i: (i, 0, 0)),
        out_shape=jax.ShapeDtypeStruct((b, s, hv), BF16),
        scratch_shapes=[pltpu.VMEM((s, hv), F32), pltpu.VMEM((s, hv), F32),
                        pltpu.VMEM((hv, hk), F32), pltpu.VMEM((hv, hk), F32)]
        + [pltpu.VMEM((s, w), F32) for w in token_scratch_widths],
        compiler_params=_params("parallel"),
        name=name,
    )(*seq_inputs, *small_inputs, *const_arrays)


FF_CHUNK = 1024


def _post_kernel(x_ref, oa_ref, ob_ref, oc_ref, pg_ref, gtm_ref, scf_ref, shf_ref, gtf_ref, g2_ref,
                 wua_ref, wub_ref, wuc_ref, wo_ref, w1_ref, w2_ref, fg_ref, out_ref, *, final):
    d = x_ref.shape[2]
    gate = lambda i: _sigmoid(pg_ref[0, :, i * d:(i + 1) * d].astype(F32))
    merged = (gate(0) * _dot(oa_ref[0], wua_ref[...])
              + gate(1) * _dot(ob_ref[0], wub_ref[...])
              + gate(2) * _dot(oc_ref[0], wuc_ref[...]))
    x1 = x_ref[0] + gtm_ref[0] * _dot(merged.astype(BF16), wo_ref[...])
    ms = jnp.mean(x1 * x1, axis=-1, keepdims=True)
    y = x1 * lax.rsqrt(ms + EPS) * g2_ref[...]
    h = (y * (1.0 + scf_ref[0]) + shf_ref[0]).astype(BF16)
    acc = jnp.zeros_like(x1)
    for c0 in range(0, w1_ref.shape[1], FF_CHUNK):
        u = jnp.maximum(_dot(h, w1_ref[:, c0:c0 + FF_CHUNK]), 0.0)
        acc = acc + _dot((u * u).astype(BF16), w2_ref[c0:c0 + FF_CHUNK, :])
    x2 = x1 + gtf_ref[0] * acc
    if final:
        x2 = x2 * lax.rsqrt(jnp.mean(x2 * x2, axis=-1, keepdims=True) + EPS) * fg_ref[...]
    out_ref[0] = x2


def _post_mix_mlp(x, oa, ob, oc, pg, gtm, scf, shf, gtf, g2, weights, fg, final, tm):
    b, s, d = x.shape
    tok = lambda a: pl.BlockSpec((1, tm, a.shape[2]), lambda i, j: (i, j, 0))
    vec = pl.BlockSpec((1, 1, d), lambda i, j: (i, 0, 0))
    return pl.pallas_call(
        functools.partial(_post_kernel, final=final),
        grid=(b, s // tm),
        in_specs=[tok(x), tok(oa), tok(ob), tok(oc), tok(pg), vec, vec, vec, vec, _const_spec((1, d))]
        + [_const_spec(w.shape) for w in weights] + [_const_spec((1, d))],
        out_specs=pl.BlockSpec((1, tm, d), lambda i, j: (i, j, 0)),
        out_shape=jax.ShapeDtypeStruct((b, s, d), F32),
        compiler_params=_params("parallel", "arbitrary"),
        name="merge_out_mlp",
    )(x, oa, ob, oc, pg, gtm, scf, shf, gtf, g2.reshape(1, d), *weights, fg.reshape(1, d))


def _input_projection_groups(w):
    o_b = 2 * DA_QK_WIDTH + DA_WIDTH
    o_c = o_b + 5 * HB_WIDTH
    o_lr = o_c + 2 * GC_KEY_WIDTH + 2 * GC_WIDTH
    o_g = o_lr + 2 * GC_GATE_RANK
    wb = HB_WIDTH
    attn = w[:, :o_b]
    hgrn_qig = jnp.concatenate([w[:, o_b:o_b + wb], w[:, o_b + 3 * wb:o_b + 5 * wb]], axis=1)
    hgrn_f = w[:, o_b + wb:o_b + 3 * wb]
    gla = w[:, o_c:o_lr]
    lr = jnp.pad(w[:, o_lr:o_g], ((0, 0), (0, LANES - 2 * GC_GATE_RANK)))
    gates = w[:, o_g:]
    return [g.astype(BF16) for g in (attn, hgrn_qig, hgrn_f, gla, lr, gates)]


PROJ_OUT_DTYPES = (BF16, BF16, F32, BF16, F32, BF16)


def kernel(x, c, ada_w, ada_b, norm_mix_g, norm_mlp_g, w_in, diff_lambda, diff_subln_g, hgrn_lb_logits,
           hgrn_norm_g, gla_gate_w2, gla_gate_b, gla_norm_g, w_up_a, w_up_b, w_up_c, w_out, mlp_w1, mlp_w2,
           final_norm_g):
    b, s, d = x.shape
    tm = min(512, s)
    tq = min(2048, s)
    mod = _modulation(c, ada_w, ada_b).reshape(DEPTH, b, N_MOD, 1, d)
    lower_bounds = _lower_bounds(hgrn_lb_logits)
    hgrn_consts = _scan_constants(SCAN_CHUNK, HB_HEADS, HB_HEAD_DIM, HB_HEAD_DIM)
    gla_consts = _scan_constants(SCAN_CHUNK, GC_HEADS, GC_KEY_DIM, GC_VAL_DIM)
    for l in range(DEPTH):
        sh_m, sc_m, gt_m, sh_f, sc_f, gt_f = [mod[l, :, i] for i in range(N_MOD)]
        pa, pb, pf, pc, plr, pg = _pre_project(
            x, sc_m, sh_m, norm_mix_g[l], _input_projection_groups(w_in[l]), PROJ_OUT_DTYPES, tm)
        o_a = _diff_attention(pa, diff_lambda[l], diff_subln_g[l], l, tq)
        o_b = _scan_call(
            _hgrn_kernel, "hgrn2_scan", [pb, pf],
            [lower_bounds[l], jnp.tile(hgrn_norm_g[l], HB_HEADS).reshape(1, HB_WIDTH)],
            hgrn_consts, b, s, HB_WIDTH, HB_WIDTH, [HB_WIDTH] * 5)
        w2 = jnp.zeros((LANES, 2 * GC_KEY_WIDTH), F32)
        w2 = w2.at[0:GC_GATE_RANK, 0:GC_KEY_WIDTH].set(gla_gate_w2[l, 0])
        w2 = w2.at[GC_GATE_RANK:2 * GC_GATE_RANK, GC_KEY_WIDTH:].set(gla_gate_w2[l, 1])
        w2_hi = w2.astype(BF16)
        w2_lo = (w2 - w2_hi.astype(F32)).astype(BF16)
        o_c = _scan_call(
            _gla_kernel, "gla_scan", [pc, plr],
            [w2_hi, w2_lo, gla_gate_b[l].reshape(1, 2 * GC_KEY_WIDTH),
             jnp.tile(gla_norm_g[l], GC_HEADS).reshape(1, GC_WIDTH)],
            gla_consts, b, s, GC_KEY_WIDTH, GC_WIDTH, [2 * GC_KEY_WIDTH])
        weights = [w.astype(BF16) for w in (w_up_a[l], w_up_b[l], w_up_c[l], w_out[l], mlp_w1[l], mlp_w2[l])]
        x = _post_mix_mlp(x, o_a, o_b, o_c, pg, gt_m, sc_f, sh_f, gt_f, norm_mlp_g[l], weights,
                          final_norm_g, l == DEPTH - 1, tm)
    return x
```

```python
import functools
import math

import numpy as np
import jax
import jax.numpy as jnp
from jax import lax
from jax.experimental import pallas as pl
from jax.experimental.pallas import tpu as pltpu

F32 = jnp.float32
BF16 = jnp.bfloat16

D_MODEL = 1024
DEPTH = 4
DA_HEADS = 4
DA_HEAD_DIM = 64
DA_V_DIM = 2 * DA_HEAD_DIM
DA_QK_WIDTH = DA_HEADS * 2 * DA_HEAD_DIM
DA_WIDTH = DA_HEADS * DA_V_DIM
HB_HEADS = 4
HB_HEAD_DIM = 64
HB_WIDTH = HB_HEADS * HB_HEAD_DIM
MAX_NEG_LOGIT = 80.0
GC_HEADS = 4
GC_KEY_DIM = 32
GC_VAL_DIM = 64
GC_KEY_WIDTH = GC_HEADS * GC_KEY_DIM
GC_WIDTH = GC_HEADS * GC_VAL_DIM
GC_GATE_RANK = 16
GC_GATE_NORMALIZER = 16.0
N_BRANCH = 3
D_FF = 4 * D_MODEL
N_MOD = 6
EPS = 1e-6
LOG2_E = 1.4426950408889634
ATTN_SUB_ROWS = 256

LANES = 128
SCAN_CHUNK = 64
SCAN_CHUNKS_PER_STEP = 4
V7X_VMEM_BYTES = 64 * 1024 * 1024
VMEM_LIMIT = V7X_VMEM_BYTES - 8 * 1024 * 1024


def _dot(a, b):
    return jnp.dot(a, b, preferred_element_type=F32)


def _dot_nt(a, b):
    return lax.dot_general(a, b, (((1,), (1,)), ((), ())), preferred_element_type=F32)


def _dot_tn(a, b):
    return lax.dot_general(a, b, (((0,), (0,)), ((), ())), preferred_element_type=F32)


def _split_bf16(a):
    hi = a.astype(BF16)
    lo = (a - hi.astype(F32)).astype(BF16)
    return hi, lo


def _sigmoid(a):
    return 1.0 / (1.0 + jnp.exp(-a))


def _silu(a):
    return a * _sigmoid(a)


def _log_sigmoid(z):
    return -(jnp.maximum(-z, 0.0) + jnp.log(1.0 + jnp.exp(-jnp.abs(z))))


def _params(*sem):
    return pltpu.CompilerParams(dimension_semantics=sem, vmem_limit_bytes=VMEM_LIMIT)


def _const_spec(shape):
    n = len(shape)
    return pl.BlockSpec(shape, lambda *_: (0,) * n, pipeline_mode=pl.Buffered(1))


def _mod_kernel(c_ref, w_ref, b_ref, o_ref):
    cond = _silu(c_ref[...])
    o_ref[0] = jnp.dot(cond, w_ref[0], preferred_element_type=F32,
                       precision=lax.Precision.HIGHEST) + b_ref[0]


def _modulation(c, ada_w, ada_b):
    depth, d, n = ada_w.shape
    b = c.shape[0]
    tn = 1536
    return pl.pallas_call(
        _mod_kernel,
        grid=(depth, n // tn),
        in_specs=[
            pl.BlockSpec((b, d), lambda l, j: (0, 0)),
            pl.BlockSpec((1, d, tn), lambda l, j: (l, 0, j)),
            pl.BlockSpec((1, 1, tn), lambda l, j: (l, 0, j)),
        ],
        out_specs=pl.BlockSpec((1, b, tn), lambda l, j: (l, 0, j)),
        out_shape=jax.ShapeDtypeStruct((depth, b, n), F32),
        compiler_params=_params("arbitrary", "arbitrary"),
        name="modulation",
    )(c, ada_w, ada_b.reshape(depth, 1, n))


def _lb_kernel(z_ref, o_ref):
    z = z_ref[...]
    e = jnp.exp(z - jnp.max(z, axis=0, keepdims=True))
    w = e / jnp.sum(e, axis=0, keepdims=True)
    run = jnp.zeros_like(w[0:1])
    for l in range(z.shape[0]):
        run = run + w[l:l + 1]
        o_ref[l:l + 1, :] = run - w[0:1]


def _lower_bounds(logits):
    depth = logits.shape[0]
    flat = logits.reshape(depth, -1).astype(F32)
    out = pl.pallas_call(
        _lb_kernel,
        out_shape=jax.ShapeDtypeStruct(flat.shape, F32),
        name="hgrn_lower_bounds",
    )(flat)
    return out.reshape(logits.shape)


PROJ_COL_CHUNK = 512


def _pre_kernel(x_ref, sc_ref, sh_ref, g_ref, *refs):
    n = len(refs) // 2
    w_refs, o_refs = refs[:n], refs[n:]
    x = x_ref[0]
    ms = jnp.mean(x * x, axis=-1, keepdims=True)
    y = x * lax.rsqrt(ms + EPS) * g_ref[...]
    h = (y * (1.0 + sc_ref[0]) + sh_ref[0]).astype(BF16)
    for w_ref, o_ref in zip(w_refs, o_refs):
        width = w_ref.shape[1]
        for c0 in range(0, width, PROJ_COL_CHUNK):
            c1 = min(c0 + PROJ_COL_CHUNK, width)
            o_ref[0, :, c0:c1] = _dot(h, w_ref[:, c0:c1]).astype(o_ref.dtype)


def _pre_project(x, sc, sh, g, weights, out_dtypes, tm):
    b, s, d = x.shape
    in_specs = [
        pl.BlockSpec((1, tm, d), lambda i, j: (i, j, 0)),
        pl.BlockSpec((1, 1, d), lambda i, j: (i, 0, 0)),
        pl.BlockSpec((1, 1, d), lambda i, j: (i, 0, 0)),
        _const_spec((1, d)),
    ] + [_const_spec(w.shape) for w in weights]
    out_specs = [pl.BlockSpec((1, tm, w.shape[1]), lambda i, j: (i, j, 0)) for w in weights]
    out_shape = [jax.ShapeDtypeStruct((b, s, w.shape[1]), dt) for w, dt in zip(weights, out_dtypes)]
    return pl.pallas_call(
        _pre_kernel,
        grid=(b, s // tm),
        in_specs=in_specs,
        out_specs=out_specs,
        out_shape=out_shape,
        compiler_params=_params("parallel", "arbitrary"),
        name="prenorm_project",
    )(x, sc, sh, g.reshape(1, d), *weights)


def _attn_kernel(q_ref, k_ref, v_ref, slope_ref, lam_ref, g_ref, o_ref, bias_ref, *, lam_init, tq):
    qi = pl.program_id(0)

    @pl.when(pl.program_id(2) == 0)
    def _():
        s = bias_ref.shape[1]
        row = lax.broadcasted_iota(jnp.int32, (tq, s), 0) + qi * tq
        col = lax.broadcasted_iota(jnp.int32, (tq, s), 1)
        dist = jnp.abs(row - col).astype(F32)
        bias_ref[...] = -(slope_ref[0][:, 0:1] * LOG2_E) * dist

    first = lax.broadcasted_iota(jnp.int32, (1, 2 * DA_HEAD_DIM), 1) < DA_HEAD_DIM
    k = k_ref[0]
    v = v_ref[0]
    sub = min(ATTN_SUB_ROWS, tq)

    def scores(r0):
        qs = (q_ref[0, r0:r0 + sub, :].astype(F32) * (DA_HEAD_DIM ** -0.5 * LOG2_E)).astype(BF16)
        zero = jnp.zeros_like(qs)
        return _dot_nt(jnp.where(first, qs, zero), k), _dot_nt(jnp.where(first, zero, qs), k)

    v_ones = jnp.concatenate([v, jnp.ones_like(v)], axis=1)

    def softmax_v(sc, r0):
        m = jnp.max(sc + bias_ref[r0:r0 + sub, :], axis=-1, keepdims=True)
        pv = _dot(jnp.exp2((sc - m) + bias_ref[r0:r0 + sub, :]).astype(BF16), v_ones)
        return pv[:, :DA_V_DIM] / pv[:, DA_V_DIM:DA_V_DIM + 1]

    lp = lam_ref[...]
    lam = (jnp.exp(jnp.sum(lp[0:1] * lp[1:2], axis=-1, keepdims=True))
           - jnp.exp(jnp.sum(lp[2:3] * lp[3:4], axis=-1, keepdims=True)) + lam_init)
    nxt = scores(0)
    for r0 in range(0, tq, sub):
        sc1, sc2 = nxt
        if r0 + sub < tq:
            nxt = scores(r0 + sub)
        o = softmax_v(sc1, r0) - lam * softmax_v(sc2, r0)
        y = o * lax.rsqrt(jnp.mean(o * o, axis=-1, keepdims=True) + EPS) * g_ref[...]
        o_ref[0, r0:r0 + sub, :] = (y * (1.0 - lam_init)).astype(o_ref.dtype)


def _diff_attention(pa, lam_params, subln_g, layer_idx, tq):
    b, s, _ = pa.shape
    h = DA_HEADS
    w = DA_V_DIM
    lam_init = 0.8 - 0.6 * math.exp(-0.3 * layer_idx)
    slopes = np.array([2.0 ** (-8.0 * (i + 1) / h) for i in range(h)], np.float32)
    slopes = jnp.asarray(np.broadcast_to(slopes[:, None, None], (h, 1, LANES)).copy())
    return pl.pallas_call(
        functools.partial(_attn_kernel, lam_init=lam_init, tq=tq),
        grid=(s // tq, h, b),
        in_specs=[
            pl.BlockSpec((1, tq, w), lambda qi, hh, bb: (bb, qi, hh)),
            pl.BlockSpec((1, s, w), lambda qi, hh, bb: (bb, 0, h + hh)),
            pl.BlockSpec((1, s, w), lambda qi, hh, bb: (bb, 0, 2 * h + hh)),
            pl.BlockSpec((1, 1, LANES), lambda qi, hh, bb: (hh, 0, 0)),
            pl.BlockSpec(lam_params.shape, lambda qi, hh, bb: (0, 0)),
            pl.BlockSpec((1, w), lambda qi, hh, bb: (0, 0)),
        ],
        out_specs=pl.BlockSpec((1, tq, w), lambda qi, hh, bb: (bb, qi, hh)),
        out_shape=jax.ShapeDtypeStruct((b, s, h * w), BF16),
        scratch_shapes=[pltpu.VMEM((tq, s), F32)],
        compiler_params=_params("arbitrary", "arbitrary", "arbitrary"),
        name="diff_attention",
    )(pa, pa, pa, slopes, lam_params.astype(F32), subln_g.reshape(1, w).astype(F32))


def _scan_constants(c, h, dk, dv):
    n_lev = int(math.log2(c))
    assert 2 ** n_lev == c
    assert dv & (dv - 1) == 0, "1/dv must be exact in bf16"
    t = np.arange(c)[:, None]
    r = np.arange(c)[None, :]
    blocks = [(r <= t), (r > t)]
    lev = np.full((c, c), -1, np.int32)
    lev[np.arange(c), np.arange(c)] = n_lev
    for li in range(n_lev):
        bs = c >> (li + 1)
        tb, rb = t // bs, r // bs
        blocks.append((rb == tb) & np.where(tb % 2 == 1, r <= t, r > t))
        lev[(tb % 2 == 1) & (rb == tb - 1)] = li
    blocks.append(np.ones((8, c), bool))
    flip = lambda m: m[::-1, ::-1]
    twice = lambda ms: np.tile(np.concatenate(ms, axis=0).astype(np.float32), (1, 2))
    w_f = twice(blocks)
    w_b = twice([flip(m) for m in blocks])
    bd =((np.arange(h * dv) // dv)[:, None] == (np.arange(h * dk) // dk)[None, :]).astype(np.float32)
    bdn = ((np.arange(h * dv) // dv)[:, None] == (np.arange(h * dv) // dv)[None, :]).astype(np.float32) / dv
    return dict(
        w_f=jnp.asarray(w_f, BF16), w_b=jnp.asarray(w_b, BF16),
        lev_f=jnp.asarray(np.tile(lev, (1, 2))), lev_b=jnp.asarray(np.tile(flip(lev), (1, 2))),
        bd=jnp.asarray(bd), bdn=jnp.asarray(bdn, BF16),
    )


SCAN_CONST_NAMES = ("w_f", "w_b", "lev_f", "lev_b", "bd", "bdn")


def _scan_chunks(chains, bd_ref, n_heads):
    c = chains[0][0].shape[0]
    n_lev = int(math.log2(c))
    hv = chains[0][2].shape[1]
    assert 2 * c == LANES and 2 * (hv // n_heads) == LANES and n_heads % 2 == 0
    blk = lambda ex, i: ex[i * c:(i + 1) * c]
    lane = lax.broadcasted_iota(jnp.int32, (1, LANES), 1)
    half = [jnp.where(lane < LANES // 2, 1.0, 0.0).astype(BF16), jnp.where(lane < LANES // 2, 0.0, 1.0).astype(BF16)]

    def scores(qa, ka):
        kt = jnp.concatenate([ka, ka], axis=0).T
        dk = kt.shape[0] // n_heads
        cols = []
        for g in range(n_heads // 2):
            cols.append(jnp.concatenate(
                [kt[h * dk:(h + 1) * dk] * half[h % 2] if h // 2 == g else jnp.zeros((dk, LANES), BF16)
                 for h in range(n_heads)], axis=0))
        return _dot(qa, jnp.concatenate(cols, axis=1))

    def value_stack(v):
        rows = []
        for h in range(n_heads):
            rows.append(jnp.concatenate(
                [v[:, g * LANES:(g + 1) * LANES] * half[h % 2] if h // 2 == g else jnp.zeros((c, LANES), BF16)
                 for g in range(hv // LANES)], axis=1))
        return jnp.concatenate(rows, axis=0)

    exs = [None] * len(chains)
    for first in range(len(chains)):
        w_ref = chains[first][5]
        if exs[first] is not None:
            continue
        idx = [i for i, ch in enumerate(chains) if ch[5] is w_ref]
        parts = [_split_bf16(chains[i][3]) for i in idx]
        rhs = jnp.concatenate([jnp.concatenate([hi, lo], axis=0) for hi, lo in parts], axis=1)
        ex = jnp.exp2(_dot(w_ref[...], rhs))
        width = chains[idx[0]][3].shape[1]
        for n, i in enumerate(idx):
            exs[i] = ex[:, n * width:(n + 1) * width]
    def owned(ch, li):
        own = ch[6][...] == li
        return jnp.concatenate([own] * (n_heads // 2), axis=1)

    accs = [jnp.where(owned(ch, n_lev), scores(ch[0], ch[1]), 0.0) for ch in chains]
    for li in range(n_lev):
        for i, (ch, ex) in enumerate(zip(chains, exs)):
            e = blk(ex, 2 + li).astype(BF16)
            accs[i] = jnp.where(owned(ch, li), scores(ch[0] * e, ch[1] * e), accs[i])
    outs = [_dot(acc.astype(BF16), value_stack(ch[2])) for ch, acc in zip(chains, accs)]
    news = [_dot_tn(ch[2], ch[1] * blk(ex, 1).astype(BF16)) for ch, ex in zip(chains, exs)]
    for i, (ch, ex) in enumerate(zip(chains, exs)):
        st_ref = ch[4]
        st = st_ref[...]
        outs[i] = outs[i] + _dot_nt(ch[0] * blk(ex, 0).astype(BF16), st.astype(BF16))
        e_tot = ex[(2 + n_lev) * c:(2 + n_lev) * c + 1]
        st_ref[...] = st * e_tot + news[i] * bd_ref[...]
    return outs


def _head_rmsnorm_gate(o, gain, gate, bdn):
    ms = _dot((o * o).astype(BF16), bdn)
    return o * lax.rsqrt(ms + EPS) * gain * _silu(gate)


def _bidir_scan(prologue, prep_f, prep_b, gate_fn, gain_ref, consts, o_ref, of_ref, ob_ref, stf_ref,
                stb_ref, n_heads):
    w_f, w_b, lev_f, lev_b, bd, bdn = consts
    s = of_ref.shape[0]
    c = SCAN_CHUNK
    group = SCAN_CHUNKS_PER_STEP
    block = group * c
    assert s % block == 0
    nb = s // block
    stf_ref[...] = jnp.zeros_like(stf_ref)
    stb_ref[...] = jnp.zeros_like(stb_ref)

    def prologue_step(jb, carry):
        prologue(pl.multiple_of(jb * block, block), block)
        return carry

    lax.fori_loop(0, nb, prologue_step, 0, unroll=2 if nb % 2 == 0 else 1)

    def step(j, carry):
        rows, chains = [], []
        for g in range(group):
            rf = pl.multiple_of(j * block + g * c, c)
            rb = pl.multiple_of((nb - 1 - j) * block + (group - 1 - g) * c, c)
            rows += [(of_ref, rf), (ob_ref, rb)]
            chains += [prep_f(rf) + (stf_ref, w_f, lev_f), prep_b(rb) + (stb_ref, w_b, lev_b)]
        outs = _scan_chunks(chains, bd, n_heads)
        for (ref, r0), o in zip(rows, outs):
            ref[pl.ds(r0, c), :] = o
        return carry

    lax.fori_loop(0, nb, step, 0)
    for jb in range(nb):
        rows = pl.ds(jb * block, block)
        y = _head_rmsnorm_gate(of_ref[rows, :] + ob_ref[rows, :], gain_ref[...],
                               gate_fn(jb * block, block), bdn[...])
        o_ref[0, rows, :] = y.astype(o_ref.dtype)


def _hgrn_kernel(pb_ref, pf_ref, lb_ref, gain_ref, *refs):
    consts = refs[:len(SCAN_CONST_NAMES)]
    (o_ref, of_ref, ob_ref, stf_ref, stb_ref, q_ref, kf_ref, kb_ref, laf_ref,
     lab_ref) = refs[len(SCAN_CONST_NAMES):]
    k_refs = (kf_ref, kb_ref)
    la_refs = (laf_ref, lab_ref)
    c = SCAN_CHUNK
    wd = HB_WIDTH

    def prologue(r0, rows):
        q_ref[pl.ds(r0, rows), :] = (_silu(pb_ref[0, pl.ds(r0, rows), 0:wd].astype(F32))
                                     * HB_HEAD_DIM ** -0.5).astype(q_ref.dtype)
        for direction in range(2):
            lb = lb_ref[direction:direction + 1, :]
            z = pf_ref[0, pl.ds(r0, rows), direction * wd:(direction + 1) * wd]
            nz2 = z * -LOG2_E
            e = jnp.exp2(-jnp.abs(nz2))
            r = 1.0 / (1.0 + e)
            cap = jnp.exp2(jnp.minimum(nz2, MAX_NEG_LOGIT * LOG2_E))
            la_refs[direction][pl.ds(r0, rows), :] = (
                jnp.log((1.0 + lb * cap) * r) * LOG2_E - jnp.maximum(nz2, 0.0))
            k_refs[direction][pl.ds(r0, rows), :] = (
                (1.0 - lb) * jnp.where(z >= 0.0, e * r, r)).astype(k_refs[direction].dtype)

    def prep(direction):
        def fn(r0):
            rows = pl.ds(r0, c)
            return (q_ref[rows, :], k_refs[direction][rows, :], pb_ref[0, rows, wd:2 * wd],
                    la_refs[direction][rows, :])
        return fn

    gate_fn = lambda r0, rows: pb_ref[0, pl.ds(r0, rows), 2 * wd:3 * wd].astype(F32)
    _bidir_scan(prologue, prep(0), prep(1), gate_fn, gain_ref, consts, o_ref, of_ref, ob_ref, stf_ref,
                stb_ref, HB_HEADS)


def _gla_kernel(pc_ref, plr_ref, w2h_ref, w2l_ref, gb_ref, gain_ref, *refs):
    consts = refs[:len(SCAN_CONST_NAMES)]
    o_ref, of_ref, ob_ref, stf_ref, stb_ref, la_ref = refs[len(SCAN_CONST_NAMES):]
    c = SCAN_CHUNK
    kw = GC_KEY_WIDTH
    vw = GC_WIDTH

    def prologue(r0, rows):
        lr_hi, lr_lo = _split_bf16(plr_ref[0, pl.ds(r0, rows), :])
        w2h = w2h_ref[...]
        z = _dot(lr_hi, w2h) + _dot(lr_lo, w2h) + _dot(lr_hi, w2l_ref[...]) + gb_ref[...]
        la_ref[pl.ds(r0, rows), :] = _log_sigmoid(z) * (LOG2_E / GC_GATE_NORMALIZER)

    def prep(direction):
        def fn(r0):
            rows = pl.ds(r0, c)
            qkv = pc_ref[0, rows, :]
            q = (qkv[:, 0:kw].astype(F32) * GC_KEY_DIM ** -0.5).astype(BF16)
            k = qkv[:, kw:2 * kw]
            v = qkv[:, 2 * kw:2 * kw + vw]
            return q, k, v, la_ref[rows, direction * kw:(direction + 1) * kw]
        return fn

    gate_fn = lambda r0, rows: pc_ref[0, pl.ds(r0, rows), 2 * kw + vw:2 * kw + 2 * vw].astype(F32)
    _bidir_scan(prologue, prep(0), prep(1), gate_fn, gain_ref, consts, o_ref, of_ref, ob_ref, stf_ref,
                stb_ref, GC_HEADS)


def _scan_call(kernel, name, seq_inputs, small_inputs, consts, b, s, hk, hv, token_scratch):
    const_arrays = [consts[n] for n in SCAN_CONST_NAMES]
    in_specs = ([pl.BlockSpec((1, s, a.shape[2]), lambda i: (i, 0, 0)) for a in seq_inputs]
                + [_const_spec(a.shape) for a in small_inputs]
                + [_const_spec(a.shape) for a in const_arrays])
    return pl.pallas_call(
        kernel,
        grid=(b,),
        in_specs=in_specs,
        out_specs=pl.BlockSpec((1, s, hv), lambda i: (i, 0, 0)),
        out_shape=jax.ShapeDtypeStruct((b, s, hv), BF16),
        scratch_shapes=[pltpu.VMEM((s, hv), F32), pltpu.VMEM((s, hv), F32),
                        pltpu.VMEM((hv, hk), F32), pltpu.VMEM((hv, hk), F32)]
        + [pltpu.VMEM((s, w), dt) for w, dt in token_scratch],
        compiler_params=_params("parallel"),
        name=name,
    )(*seq_inputs, *small_inputs, *const_arrays)


FF_CHUNK = 1024


def _post_kernel(x_ref, oa_ref, ob_ref, oc_ref, pg_ref, gtm_ref, scf_ref, shf_ref, gtf_ref, g2_ref,
                 wua_ref, wub_ref, wuc_ref, wo_ref, w1_ref, w2_ref, fg_ref, out_ref, *, final):
    d = x_ref.shape[2]
    gate = lambda i: _sigmoid(pg_ref[0, :, i * d:(i + 1) * d].astype(F32))
    merged = (gate(0) * _dot(oa_ref[0], wua_ref[...])
              + gate(1) * _dot(ob_ref[0], wub_ref[...])
              + gate(2) * _dot(oc_ref[0], wuc_ref[...]))
    x1 = x_ref[0] + gtm_ref[0] * _dot(merged.astype(BF16), wo_ref[...])
    ms = jnp.mean(x1 * x1, axis=-1, keepdims=True)
    y = x1 * lax.rsqrt(ms + EPS) * g2_ref[...]
    h = (y * (1.0 + scf_ref[0]) + shf_ref[0]).astype(BF16)
    acc = jnp.zeros_like(x1)
    for c0 in range(0, w1_ref.shape[1], FF_CHUNK):
        u = jnp.maximum(_dot(h, w1_ref[:, c0:c0 + FF_CHUNK]), 0.0)
        acc = acc + _dot((u * u).astype(BF16), w2_ref[c0:c0 + FF_CHUNK, :])
    x2 = x1 + gtf_ref[0] * acc
    if final:
        x2 = x2 * lax.rsqrt(jnp.mean(x2 * x2, axis=-1, keepdims=True) + EPS) * fg_ref[...]
    out_ref[0] = x2


def _post_mix_mlp(x, oa, ob, oc, pg, gtm, scf, shf, gtf, g2, weights, fg, final, tm):
    b, s, d = x.shape
    tok = lambda a: pl.BlockSpec((1, tm, a.shape[2]), lambda i, j: (i, j, 0))
    vec = pl.BlockSpec((1, 1, d), lambda i, j: (i, 0, 0))
    return pl.pallas_call(
        functools.partial(_post_kernel, final=final),
        grid=(b, s // tm),
        in_specs=[tok(x), tok(oa), tok(ob), tok(oc), tok(pg), vec, vec, vec, vec, _const_spec((1, d))]
        + [_const_spec(w.shape) for w in weights] + [_const_spec((1, d))],
        out_specs=pl.BlockSpec((1, tm, d), lambda i, j: (i, j, 0)),
        out_shape=jax.ShapeDtypeStruct((b, s, d), F32),
        compiler_params=_params("parallel", "arbitrary"),
        name="merge_out_mlp",
    )(x, oa, ob, oc, pg, gtm, scf, shf, gtf, g2.reshape(1, d), *weights, fg.reshape(1, d))


def _input_projection_groups(w):
    o_b = 2 * DA_QK_WIDTH + DA_WIDTH
    o_c = o_b + 5 * HB_WIDTH
    o_lr = o_c + 2 * GC_KEY_WIDTH + 2 * GC_WIDTH
    o_g = o_lr + 2 * GC_GATE_RANK
    wb = HB_WIDTH
    attn = w[:, :o_b]
    hgrn_qig = jnp.concatenate([w[:, o_b:o_b + wb], w[:, o_b + 3 * wb:o_b + 5 * wb]], axis=1)
    hgrn_f = w[:, o_b + wb:o_b + 3 * wb]
    gla = w[:, o_c:o_lr]
    lr = jnp.pad(w[:, o_lr:o_g], ((0, 0), (0, LANES - 2 * GC_GATE_RANK)))
    gates = w[:, o_g:]
    return [g.astype(BF16) for g in (attn, hgrn_qig, hgrn_f, gla, lr, gates)]


PROJ_OUT_DTYPES = (BF16, BF16, F32, BF16, F32, BF16)


def kernel(x, c, ada_w, ada_b, norm_mix_g, norm_mlp_g, w_in, diff_lambda, diff_subln_g, hgrn_lb_logits,
           hgrn_norm_g, gla_gate_w2, gla_gate_b, gla_norm_g, w_up_a, w_up_b, w_up_c, w_out, mlp_w1, mlp_w2,
           final_norm_g):
    b, s, d = x.shape
    tm = min(512, s)
    tq = min(2048, s)
    mod = _modulation(c, ada_w, ada_b).reshape(DEPTH, b, N_MOD, 1, d)
    lower_bounds = _lower_bounds(hgrn_lb_logits)
    hgrn_consts = _scan_constants(SCAN_CHUNK, HB_HEADS, HB_HEAD_DIM, HB_HEAD_DIM)
    gla_consts = _scan_constants(SCAN_CHUNK, GC_HEADS, GC_KEY_DIM, GC_VAL_DIM)
    for l in range(DEPTH):
        sh_m, sc_m, gt_m, sh_f, sc_f, gt_f = [mod[l, :, i] for i in range(N_MOD)]
        pa, pb, pf, pc, plr, pg = _pre_project(
            x, sc_m, sh_m, norm_mix_g[l], _input_projection_groups(w_in[l]), PROJ_OUT_DTYPES, tm)
        o_a = _diff_attention(pa, diff_lambda[l], diff_subln_g[l], l, tq)
        o_b = _scan_call(
            _hgrn_kernel, "hgrn2_scan", [pb, pf],
            [lower_bounds[l], jnp.tile(hgrn_norm_g[l], HB_HEADS).reshape(1, HB_WIDTH)],
            hgrn_consts, b, s, HB_WIDTH, HB_WIDTH,
            [(HB_WIDTH, BF16)] * 3 + [(HB_WIDTH, F32)] * 2)
        w2 = jnp.zeros((LANES, 2 * GC_KEY_WIDTH), F32)
        w2 = w2.at[0:GC_GATE_RANK, 0:GC_KEY_WIDTH].set(gla_gate_w2[l, 0])
        w2 = w2.at[GC_GATE_RANK:2 * GC_GATE_RANK, GC_KEY_WIDTH:].set(gla_gate_w2[l, 1])
        w2_hi = w2.astype(BF16)
        w2_lo = (w2 - w2_hi.astype(F32)).astype(BF16)
        o_c = _scan_call(
            _gla_kernel, "gla_scan", [pc, plr],
            [w2_hi, w2_lo, gla_gate_b[l].reshape(1, 2 * GC_KEY_WIDTH),
             jnp.tile(gla_norm_g[l], GC_HEADS).reshape(1, GC_WIDTH)],
            gla_consts, b, s, GC_KEY_WIDTH, GC_WIDTH, [(2 * GC_KEY_WIDTH, F32)])
        weights = [w.astype(BF16) for w in (w_up_a[l], w_up_b[l], w_up_c[l], w_out[l], mlp_w1[l], mlp_w2[l])]
        x = _post_mix_mlp(x, o_a, o_b, o_c, pg, gt_m, sc_f, sh_f, gt_f, norm_mlp_g[l], weights,
                          final_norm_g, l == DEPTH - 1, tm)
    return x
```

```python
import functools
import math

import numpy as np
import jax
import jax.numpy as jnp
from jax import lax
from jax.experimental import pallas as pl
from jax.experimental.pallas import tpu as pltpu

F32 = jnp.float32
BF16 = jnp.bfloat16

D_MODEL = 1024
DEPTH = 4
DA_HEADS = 4
DA_HEAD_DIM = 64
DA_V_DIM = 2 * DA_HEAD_DIM
DA_QK_WIDTH = DA_HEADS * 2 * DA_HEAD_DIM
DA_WIDTH = DA_HEADS * DA_V_DIM
HB_HEADS = 4
HB_HEAD_DIM = 64
HB_WIDTH = HB_HEADS * HB_HEAD_DIM
MAX_NEG_LOGIT = 80.0
GC_HEADS = 4
GC_KEY_DIM = 32
GC_VAL_DIM = 64
GC_KEY_WIDTH = GC_HEADS * GC_KEY_DIM
GC_WIDTH = GC_HEADS * GC_VAL_DIM
GC_GATE_RANK = 16
GC_GATE_NORMALIZER = 16.0
N_BRANCH = 3
D_FF = 4 * D_MODEL
N_MOD = 6
EPS = 1e-6
LOG2_E = 1.4426950408889634
ATTN_SUB_ROWS = 256

LANES = 128
SCAN_CHUNK = 64
SCAN_CHUNKS_PER_STEP = 8
V7X_VMEM_BYTES = 64 * 1024 * 1024
VMEM_LIMIT = V7X_VMEM_BYTES - 8 * 1024 * 1024


def _dot(a, b):
    return jnp.dot(a, b, preferred_element_type=F32)


def _dot_nt(a, b):
    return lax.dot_general(a, b, (((1,), (1,)), ((), ())), preferred_element_type=F32)


def _dot_tn(a, b):
    return lax.dot_general(a, b, (((0,), (0,)), ((), ())), preferred_element_type=F32)


def _split_bf16(a):
    hi = a.astype(BF16)
    lo = (a - hi.astype(F32)).astype(BF16)
    return hi, lo


def _sigmoid(a):
    return 1.0 / (1.0 + jnp.exp(-a))


def _silu(a):
    return a * _sigmoid(a)


def _log_sigmoid(z):
    return -(jnp.maximum(-z, 0.0) + jnp.log(1.0 + jnp.exp(-jnp.abs(z))))


def _params(*sem):
    return pltpu.CompilerParams(dimension_semantics=sem, vmem_limit_bytes=VMEM_LIMIT)


def _const_spec(shape):
    n = len(shape)
    return pl.BlockSpec(shape, lambda *_: (0,) * n, pipeline_mode=pl.Buffered(1))


def _mod_kernel(c_ref, w_ref, b_ref, o_ref):
    cond = _silu(c_ref[...])
    o_ref[0] = jnp.dot(cond, w_ref[0], preferred_element_type=F32,
                       precision=lax.Precision.HIGHEST) + b_ref[0]


def _modulation(c, ada_w, ada_b):
    depth, d, n = ada_w.shape
    b = c.shape[0]
    tn = 1536
    return pl.pallas_call(
        _mod_kernel,
        grid=(depth, n // tn),
        in_specs=[
            pl.BlockSpec((b, d), lambda l, j: (0, 0)),
            pl.BlockSpec((1, d, tn), lambda l, j: (l, 0, j)),
            pl.BlockSpec((1, 1, tn), lambda l, j: (l, 0, j)),
        ],
        out_specs=pl.BlockSpec((1, b, tn), lambda l, j: (l, 0, j)),
        out_shape=jax.ShapeDtypeStruct((depth, b, n), F32),
        compiler_params=_params("arbitrary", "arbitrary"),
        name="modulation",
    )(c, ada_w, ada_b.reshape(depth, 1, n))


def _lb_kernel(z_ref, o_ref):
    z = z_ref[...]
    e = jnp.exp(z - jnp.max(z, axis=0, keepdims=True))
    w = e / jnp.sum(e, axis=0, keepdims=True)
    run = jnp.zeros_like(w[0:1])
    for l in range(z.shape[0]):
        run = run + w[l:l + 1]
        o_ref[l:l + 1, :] = run - w[0:1]


def _lower_bounds(logits):
    depth = logits.shape[0]
    flat = logits.reshape(depth, -1).astype(F32)
    out = pl.pallas_call(
        _lb_kernel,
        out_shape=jax.ShapeDtypeStruct(flat.shape, F32),
        name="hgrn_lower_bounds",
    )(flat)
    return out.reshape(logits.shape)


PROJ_COL_CHUNK = 512


def _pre_kernel(x_ref, sc_ref, sh_ref, g_ref, *refs):
    n = len(refs) // 2
    w_refs, o_refs = refs[:n], refs[n:]
    x = x_ref[0]
    ms = jnp.mean(x * x, axis=-1, keepdims=True)
    y = x * lax.rsqrt(ms + EPS) * g_ref[...]
    h = (y * (1.0 + sc_ref[0]) + sh_ref[0]).astype(BF16)
    for w_ref, o_ref in zip(w_refs, o_refs):
        width = w_ref.shape[1]
        for c0 in range(0, width, PROJ_COL_CHUNK):
            c1 = min(c0 + PROJ_COL_CHUNK, width)
            o_ref[0, :, c0:c1] = _dot(h, w_ref[:, c0:c1]).astype(o_ref.dtype)


def _pre_project(x, sc, sh, g, weights, out_dtypes, tm):
    b, s, d = x.shape
    in_specs = [
        pl.BlockSpec((1, tm, d), lambda i, j: (i, j, 0)),
        pl.BlockSpec((1, 1, d), lambda i, j: (i, 0, 0)),
        pl.BlockSpec((1, 1, d), lambda i, j: (i, 0, 0)),
        _const_spec((1, d)),
    ] + [_const_spec(w.shape) for w in weights]
    out_specs = [pl.BlockSpec((1, tm, w.shape[1]), lambda i, j: (i, j, 0)) for w in weights]
    out_shape = [jax.ShapeDtypeStruct((b, s, w.shape[1]), dt) for w, dt in zip(weights, out_dtypes)]
    return pl.pallas_call(
        _pre_kernel,
        grid=(b, s // tm),
        in_specs=in_specs,
        out_specs=out_specs,
        out_shape=out_shape,
        compiler_params=_params("parallel", "arbitrary"),
        name="prenorm_project",
    )(x, sc, sh, g.reshape(1, d), *weights)


def _attn_kernel(q_ref, k_ref, v_ref, slope_ref, lam_ref, g_ref, o_ref, bias_ref, *, lam_init, tq):
    qi = pl.program_id(0)

    @pl.when(pl.program_id(2) == 0)
    def _():
        s = bias_ref.shape[1]
        row = lax.broadcasted_iota(jnp.int32, (tq, s), 0) + qi * tq
        col = lax.broadcasted_iota(jnp.int32, (tq, s), 1)
        dist = jnp.abs(row - col).astype(F32)
        bias_ref[...] = -(slope_ref[0][:, 0:1] * LOG2_E) * dist

    first = lax.broadcasted_iota(jnp.int32, (1, 2 * DA_HEAD_DIM), 1) < DA_HEAD_DIM
    k = k_ref[0]
    v = v_ref[0]
    sub = min(ATTN_SUB_ROWS, tq)

    def scores(r0):
        qs = (q_ref[0, r0:r0 + sub, :].astype(F32) * (DA_HEAD_DIM ** -0.5 * LOG2_E)).astype(BF16)
        zero = jnp.zeros_like(qs)
        return _dot_nt(jnp.where(first, qs, zero), k), _dot_nt(jnp.where(first, zero, qs), k)

    v_ones = jnp.concatenate([v, jnp.ones_like(v)], axis=1)

    def softmax_v(sc, r0):
        m = jnp.max(sc + bias_ref[r0:r0 + sub, :], axis=-1, keepdims=True)
        pv = _dot(jnp.exp2((sc - m) + bias_ref[r0:r0 + sub, :]).astype(BF16), v_ones)
        return pv[:, :DA_V_DIM] / pv[:, DA_V_DIM:DA_V_DIM + 1]

    lp = lam_ref[...]
    lam = (jnp.exp(jnp.sum(lp[0:1] * lp[1:2], axis=-1, keepdims=True))
           - jnp.exp(jnp.sum(lp[2:3] * lp[3:4], axis=-1, keepdims=True)) + lam_init)
    nxt = scores(0)
    for r0 in range(0, tq, sub):
        sc1, sc2 = nxt
        if r0 + sub < tq:
            nxt = scores(r0 + sub)
        o = softmax_v(sc1, r0) - lam * softmax_v(sc2, r0)
        y = o * lax.rsqrt(jnp.mean(o * o, axis=-1, keepdims=True) + EPS) * g_ref[...]
        o_ref[0, r0:r0 + sub, :] = (y * (1.0 - lam_init)).astype(o_ref.dtype)


def _diff_attention(pa, lam_params, subln_g, layer_idx, tq):
    b, s, _ = pa.shape
    h = DA_HEADS
    w = DA_V_DIM
    lam_init = 0.8 - 0.6 * math.exp(-0.3 * layer_idx)
    slopes = np.array([2.0 ** (-8.0 * (i + 1) / h) for i in range(h)], np.float32)
    slopes = jnp.asarray(np.broadcast_to(slopes[:, None, None], (h, 1, LANES)).copy())
    return pl.pallas_call(
        functools.partial(_attn_kernel, lam_init=lam_init, tq=tq),
        grid=(s // tq, h, b),
        in_specs=[
            pl.BlockSpec((1, tq, w), lambda qi, hh, bb: (bb, qi, hh)),
            pl.BlockSpec((1, s, w), lambda qi, hh, bb: (bb, 0, h + hh)),
            pl.BlockSpec((1, s, w), lambda qi, hh, bb: (bb, 0, 2 * h + hh)),
            pl.BlockSpec((1, 1, LANES), lambda qi, hh, bb: (hh, 0, 0)),
            pl.BlockSpec(lam_params.shape, lambda qi, hh, bb: (0, 0)),
            pl.BlockSpec((1, w), lambda qi, hh, bb: (0, 0)),
        ],
        out_specs=pl.BlockSpec((1, tq, w), lambda qi, hh, bb: (bb, qi, hh)),
        out_shape=jax.ShapeDtypeStruct((b, s, h * w), BF16),
        scratch_shapes=[pltpu.VMEM((tq, s), F32)],
        compiler_params=_params("arbitrary", "arbitrary", "arbitrary"),
        name="diff_attention",
    )(pa, pa, pa, slopes, lam_params.astype(F32), subln_g.reshape(1, w).astype(F32))


def _scan_constants(c, h, dk, dv):
    n_lev = int(math.log2(c))
    assert 2 ** n_lev == c
    assert dv & (dv - 1) == 0, "1/dv must be exact in bf16"
    t = np.arange(c)[:, None]
    r = np.arange(c)[None, :]
    blocks = [(r <= t), (r > t)]
    lev = np.full((c, c), -1, np.int32)
    lev[np.arange(c), np.arange(c)] = n_lev
    for li in range(n_lev):
        bs = c >> (li + 1)
        tb, rb = t // bs, r // bs
        blocks.append((rb == tb) & np.where(tb % 2 == 1, r <= t, r > t))
        lev[(tb % 2 == 1) & (rb == tb - 1)] = li
    blocks.append(np.ones((8, c), bool))
    flip = lambda m: m[::-1, ::-1]
    twice = lambda ms: np.tile(np.concatenate(ms, axis=0).astype(np.float32), (1, 2))
    w_f = twice(blocks)
    w_b = twice([flip(m) for m in blocks])
    bdn = ((np.arange(h * dv) // dv)[:, None] == (np.arange(h * dv) // dv)[None, :]).astype(np.float32) / dv
    return dict(
        w_f=jnp.asarray(w_f, BF16), w_b=jnp.asarray(w_b, BF16),
        lev_f=jnp.asarray(np.tile(lev, (1, 2))), lev_b=jnp.asarray(np.tile(flip(lev), (1, 2))),
        bdn=jnp.asarray(bdn, BF16),
    )


SCAN_CONST_NAMES = ("w_f", "w_b", "lev_f", "lev_b", "bdn")


def _scan_chunks(chains, n_heads):
    c = chains[0][0].shape[0]
    n_lev = int(math.log2(c))
    hv = chains[0][2].shape[1]
    dv = hv // n_heads
    assert 2 * c == LANES and 2 * dv == LANES and n_heads % 2 == 0
    blk = lambda ex, i: ex[i * c:(i + 1) * c]
    lane = lax.broadcasted_iota(jnp.int32, (1, LANES), 1)
    half = [jnp.where(lane < LANES // 2, 1.0, 0.0).astype(BF16), jnp.where(lane < LANES // 2, 0.0, 1.0).astype(BF16)]

    def head_stack(a):
        r, w = a.shape
        wh = w // n_heads
        per_tile = LANES // wh
        rows = []
        for h in range(n_heads):
            keep = jnp.where(lane // wh == h % per_tile, 1.0, 0.0).astype(BF16)
            rows.append(jnp.concatenate(
                [a[:, t * LANES:(t + 1) * LANES] * keep if t == h // per_tile else jnp.zeros((r, LANES), BF16)
                 for t in range(w // LANES)], axis=1))
        return jnp.concatenate(rows, axis=0)

    def value_rows(v):
        vt = jnp.concatenate([v, v], axis=0).T
        return jnp.concatenate(
            [jnp.where(lane < LANES // 2, vt[2 * g * dv:(2 * g + 1) * dv], vt[(2 * g + 1) * dv:(2 * g + 2) * dv])
             for g in range(n_heads // 2)], axis=1)

    def scores(qa, ka):
        kt = jnp.concatenate([ka, ka], axis=0).T
        dk = kt.shape[0] // n_heads
        cols = []
        for g in range(n_heads // 2):
            cols.append(jnp.concatenate(
                [kt[h * dk:(h + 1) * dk] * half[h % 2] if h // 2 == g else jnp.zeros((dk, LANES), BF16)
                 for h in range(n_heads)], axis=0))
        return _dot(qa, jnp.concatenate(cols, axis=1))

    exs = [None] * len(chains)
    for first in range(len(chains)):
        w_ref = chains[first][5]
        if exs[first] is not None:
            continue
        idx = [i for i, ch in enumerate(chains) if ch[5] is w_ref]
        parts = [_split_bf16(chains[i][3]) for i in idx]
        rhs = jnp.concatenate([jnp.concatenate([hi, lo], axis=0) for hi, lo in parts], axis=1)
        ex = jnp.exp2(_dot(w_ref[...], rhs))
        width = chains[idx[0]][3].shape[1]
        for n, i in enumerate(idx):
            exs[i] = ex[:, n * width:(n + 1) * width]
    def owned(ch, li):
        own = ch[6][...] == li
        return jnp.concatenate([own] * (n_heads // 2), axis=1)

    accs = [jnp.where(owned(ch, n_lev), scores(ch[0], ch[1]), 0.0) for ch in chains]
    for li in range(n_lev):
        for i, (ch, ex) in enumerate(zip(chains, exs)):
            e = blk(ex, 2 + li).astype(BF16)
            accs[i] = jnp.where(owned(ch, li), scores(ch[0] * e, ch[1] * e), accs[i])
    outs = [_dot(acc.astype(BF16), head_stack(ch[2])) for ch, acc in zip(chains, accs)]
    news = [_dot(value_rows(ch[2]), head_stack(ch[1] * blk(ex, 1).astype(BF16)))
            for ch, ex in zip(chains, exs)]
    for i, (ch, ex) in enumerate(zip(chains, exs)):
        st_ref = ch[4]
        st = st_ref[...]
        outs[i] = outs[i] + _dot_nt(ch[0] * blk(ex, 0).astype(BF16), head_stack(st.astype(BF16)))
        e_tot = ex[(2 + n_lev) * c:(2 + n_lev) * c + 1]
        st_ref[...] = st * e_tot + news[i]
    return outs


def _head_rmsnorm_gate(o, gain, gate, bdn):
    ms = _dot((o * o).astype(BF16), bdn)
    return o * lax.rsqrt(ms + EPS) * gain * _silu(gate)


def _bidir_scan(prologue, prep_f, prep_b, gate_fn, gain_ref, consts, o_ref, of_ref, ob_ref, stf_ref,
                stb_ref, n_heads):
    w_f, w_b, lev_f, lev_b, bdn = consts
    s = of_ref.shape[0]
    c = SCAN_CHUNK
    group = min(SCAN_CHUNKS_PER_STEP, s // c)
    block = group * c
    assert s % block == 0
    nb = s // block
    stf_ref[...] = jnp.zeros_like(stf_ref)
    stb_ref[...] = jnp.zeros_like(stb_ref)

    def prologue_step(jb, carry):
        prologue(pl.multiple_of(jb * block, block), block)
        return carry

    lax.fori_loop(0, nb, prologue_step, 0, unroll=2 if nb % 2 == 0 else 1)

    def step(j, carry):
        rows, chains = [], []
        for g in range(group):
            rf = pl.multiple_of(j * block + g * c, c)
            rb = pl.multiple_of((nb - 1 - j) * block + (group - 1 - g) * c, c)
            rows += [(of_ref, rf), (ob_ref, rb)]
            chains += [prep_f(rf) + (stf_ref, w_f, lev_f), prep_b(rb) + (stb_ref, w_b, lev_b)]
        outs = _scan_chunks(chains, n_heads)
        for (ref, r0), o in zip(rows, outs):
            ref[pl.ds(r0, c), :] = o
        return carry

    lax.fori_loop(0, nb, step, 0)
    for jb in range(nb):
        rows = pl.ds(jb * block, block)
        y = _head_rmsnorm_gate(of_ref[rows, :] + ob_ref[rows, :], gain_ref[...],
                               gate_fn(jb * block, block), bdn[...])
        o_ref[0, rows, :] = y.astype(o_ref.dtype)


def _hgrn_kernel(pb_ref, pf_ref, lb_ref, gain_ref, *refs):
    consts = refs[:len(SCAN_CONST_NAMES)]
    (o_ref, of_ref, ob_ref, stf_ref, stb_ref, q_ref, kf_ref, kb_ref, laf_ref,
     lab_ref) = refs[len(SCAN_CONST_NAMES):]
    k_refs = (kf_ref, kb_ref)
    la_refs = (laf_ref, lab_ref)
    c = SCAN_CHUNK
    wd = HB_WIDTH

    def prologue(r0, rows):
        q_ref[pl.ds(r0, rows), :] = (_silu(pb_ref[0, pl.ds(r0, rows), 0:wd].astype(F32))
                                     * HB_HEAD_DIM ** -0.5).astype(q_ref.dtype)
        for direction in range(2):
            lb = lb_ref[direction:direction + 1, :]
            z = pf_ref[0, pl.ds(r0, rows), direction * wd:(direction + 1) * wd]
            nz2 = z * -LOG2_E
            e = jnp.exp2(-jnp.abs(nz2))
            r = 1.0 / (1.0 + e)
            cap = jnp.exp2(jnp.minimum(nz2, MAX_NEG_LOGIT * LOG2_E))
            la_refs[direction][pl.ds(r0, rows), :] = (
                jnp.log((1.0 + lb * cap) * r) * LOG2_E - jnp.maximum(nz2, 0.0))
            k_refs[direction][pl.ds(r0, rows), :] = (
                (1.0 - lb) * jnp.where(z >= 0.0, e * r, r)).astype(k_refs[direction].dtype)

    def prep(direction):
        def fn(r0):
            rows = pl.ds(r0, c)
            return (q_ref[rows, :], k_refs[direction][rows, :], pb_ref[0, rows, wd:2 * wd],
                    la_refs[direction][rows, :])
        return fn

    gate_fn = lambda r0, rows: pb_ref[0, pl.ds(r0, rows), 2 * wd:3 * wd].astype(F32)
    _bidir_scan(prologue, prep(0), prep(1), gate_fn, gain_ref, consts, o_ref, of_ref, ob_ref, stf_ref,
                stb_ref, HB_HEADS)


def _gla_kernel(pc_ref, plr_ref, w2h_ref, w2l_ref, gb_ref, gain_ref, *refs):
    consts = refs[:len(SCAN_CONST_NAMES)]
    o_ref, of_ref, ob_ref, stf_ref, stb_ref, la_ref = refs[len(SCAN_CONST_NAMES):]
    c = SCAN_CHUNK
    kw = GC_KEY_WIDTH
    vw = GC_WIDTH

    def prologue(r0, rows):
        lr_hi, lr_lo = _split_bf16(plr_ref[0, pl.ds(r0, rows), :])
        w2h = w2h_ref[...]
        z = _dot(lr_hi, w2h) + _dot(lr_lo, w2h) + _dot(lr_hi, w2l_ref[...]) + gb_ref[...]
        la_ref[pl.ds(r0, rows), :] = _log_sigmoid(z) * (LOG2_E / GC_GATE_NORMALIZER)

    def prep(direction):
        def fn(r0):
            rows = pl.ds(r0, c)
            qkv = pc_ref[0, rows, :]
            q = (qkv[:, 0:kw].astype(F32) * GC_KEY_DIM ** -0.5).astype(BF16)
            k = qkv[:, kw:2 * kw]
            v = qkv[:, 2 * kw:2 * kw + vw]
            return q, k, v, la_ref[rows, direction * kw:(direction + 1) * kw]
        return fn

    gate_fn = lambda r0, rows: pc_ref[0, pl.ds(r0, rows), 2 * kw + vw:2 * kw + 2 * vw].astype(F32)
    _bidir_scan(prologue, prep(0), prep(1), gate_fn, gain_ref, consts, o_ref, of_ref, ob_ref, stf_ref,
                stb_ref, GC_HEADS)


def _scan_call(kernel, name, seq_inputs, small_inputs, consts, b, s, hk, hv, dv, token_scratch):
    const_arrays = [consts[n] for n in SCAN_CONST_NAMES]
    in_specs = ([pl.BlockSpec((1, s, a.shape[2]), lambda i: (i, 0, 0)) for a in seq_inputs]
                + [_const_spec(a.shape) for a in small_inputs]
                + [_const_spec(a.shape) for a in const_arrays])
    return pl.pallas_call(
        kernel,
        grid=(b,),
        in_specs=in_specs,
        out_specs=pl.BlockSpec((1, s, hv), lambda i: (i, 0, 0)),
        out_shape=jax.ShapeDtypeStruct((b, s, hv), BF16),
        scratch_shapes=[pltpu.VMEM((s, hv), F32), pltpu.VMEM((s, hv), F32),
                        pltpu.VMEM((dv, hk), F32), pltpu.VMEM((dv, hk), F32)]
        + [pltpu.VMEM((s, w), dt) for w, dt in token_scratch],
        compiler_params=_params("parallel"),
        name=name,
    )(*seq_inputs, *small_inputs, *const_arrays)


FF_CHUNK = 1024


def _post_kernel(x_ref, oa_ref, ob_ref, oc_ref, pg_ref, gtm_ref, scf_ref, shf_ref, gtf_ref, g2_ref,
                 wua_ref, wub_ref, wuc_ref, wo_ref, w1_ref, w2_ref, fg_ref, out_ref, *, final):
    d = x_ref.shape[2]
    gate = lambda i: _sigmoid(pg_ref[0, :, i * d:(i + 1) * d].astype(F32))
    merged = (gate(0) * _dot(oa_ref[0], wua_ref[...])
              + gate(1) * _dot(ob_ref[0], wub_ref[...])
              + gate(2) * _dot(oc_ref[0], wuc_ref[...]))
    x1 = x_ref[0] + gtm_ref[0] * _dot(merged.astype(BF16), wo_ref[...])
    ms = jnp.mean(x1 * x1, axis=-1, keepdims=True)
    y = x1 * lax.rsqrt(ms + EPS) * g2_ref[...]
    h = (y * (1.0 + scf_ref[0]) + shf_ref[0]).astype(BF16)
    acc = jnp.zeros_like(x1)
    for c0 in range(0, w1_ref.shape[1], FF_CHUNK):
        u = jnp.maximum(_dot(h, w1_ref[:, c0:c0 + FF_CHUNK]), 0.0)
        acc = acc + _dot((u * u).astype(BF16), w2_ref[c0:c0 + FF_CHUNK, :])
    x2 = x1 + gtf_ref[0] * acc
    if final:
        x2 = x2 * lax.rsqrt(jnp.mean(x2 * x2, axis=-1, keepdims=True) + EPS) * fg_ref[...]
    out_ref[0] = x2


def _post_mix_mlp(x, oa, ob, oc, pg, gtm, scf, shf, gtf, g2, weights, fg, final, tm):
    b, s, d = x.shape
    tok = lambda a: pl.BlockSpec((1, tm, a.shape[2]), lambda i, j: (i, j, 0))
    vec = pl.BlockSpec((1, 1, d), lambda i, j: (i, 0, 0))
    return pl.pallas_call(
        functools.partial(_post_kernel, final=final),
        grid=(b, s // tm),
        in_specs=[tok(x), tok(oa), tok(ob), tok(oc), tok(pg), vec, vec, vec, vec, _const_spec((1, d))]
        + [_const_spec(w.shape) for w in weights] + [_const_spec((1, d))],
        out_specs=pl.BlockSpec((1, tm, d), lambda i, j: (i, j, 0)),
        out_shape=jax.ShapeDtypeStruct((b, s, d), F32),
        compiler_params=_params("parallel", "arbitrary"),
        name="merge_out_mlp",
    )(x, oa, ob, oc, pg, gtm, scf, shf, gtf, g2.reshape(1, d), *weights, fg.reshape(1, d))


def _input_projection_groups(w):
    o_b = 2 * DA_QK_WIDTH + DA_WIDTH
    o_c = o_b + 5 * HB_WIDTH
    o_lr = o_c + 2 * GC_KEY_WIDTH + 2 * GC_WIDTH
    o_g = o_lr + 2 * GC_GATE_RANK
    wb = HB_WIDTH
    attn = w[:, :o_b]
    hgrn_qig = jnp.concatenate([w[:, o_b:o_b + wb], w[:, o_b + 3 * wb:o_b + 5 * wb]], axis=1)
    hgrn_f = w[:, o_b + wb:o_b + 3 * wb]
    gla = w[:, o_c:o_lr]
    lr = jnp.pad(w[:, o_lr:o_g], ((0, 0), (0, LANES - 2 * GC_GATE_RANK)))
    gates = w[:, o_g:]
    return [g.astype(BF16) for g in (attn, hgrn_qig, hgrn_f, gla, lr, gates)]


PROJ_OUT_DTYPES = (BF16, BF16, F32, BF16, F32, BF16)


def kernel(x, c, ada_w, ada_b, norm_mix_g, norm_mlp_g, w_in, diff_lambda, diff_subln_g, hgrn_lb_logits,
           hgrn_norm_g, gla_gate_w2, gla_gate_b, gla_norm_g, w_up_a, w_up_b, w_up_c, w_out, mlp_w1, mlp_w2,
           final_norm_g):
    b, s, d = x.shape
    tm = min(512, s)
    tq = min(2048, s)
    mod = _modulation(c, ada_w, ada_b).reshape(DEPTH, b, N_MOD, 1, d)
    lower_bounds = _lower_bounds(hgrn_lb_logits)
    hgrn_consts = _scan_constants(SCAN_CHUNK, HB_HEADS, HB_HEAD_DIM, HB_HEAD_DIM)
    gla_consts = _scan_constants(SCAN_CHUNK, GC_HEADS, GC_KEY_DIM, GC_VAL_DIM)
    for l in range(DEPTH):
        sh_m, sc_m, gt_m, sh_f, sc_f, gt_f = [mod[l, :, i] for i in range(N_MOD)]
        pa, pb, pf, pc, plr, pg = _pre_project(
            x, sc_m, sh_m, norm_mix_g[l], _input_projection_groups(w_in[l]), PROJ_OUT_DTYPES, tm)
        o_a = _diff_attention(pa, diff_lambda[l], diff_subln_g[l], l, tq)
        o_b = _scan_call(
            _hgrn_kernel, "hgrn2_scan", [pb, pf],
            [lower_bounds[l], jnp.tile(hgrn_norm_g[l], HB_HEADS).reshape(1, HB_WIDTH)],
            hgrn_consts, b, s, HB_WIDTH, HB_WIDTH, HB_HEAD_DIM,
            [(HB_WIDTH, BF16)] * 3 + [(HB_WIDTH, F32)] * 2)
        w2 = jnp.zeros((LANES, 2 * GC_KEY_WIDTH), F32)
        w2 = w2.at[0:GC_GATE_RANK, 0:GC_KEY_WIDTH].set(gla_gate_w2[l, 0])
        w2 = w2.at[GC_GATE_RANK:2 * GC_GATE_RANK, GC_KEY_WIDTH:].set(gla_gate_w2[l, 1])
        w2_hi = w2.astype(BF16)
        w2_lo = (w2 - w2_hi.astype(F32)).astype(BF16)
        o_c = _scan_call(
            _gla_kernel, "gla_scan", [pc, plr],
            [w2_hi, w2_lo, gla_gate_b[l].reshape(1, 2 * GC_KEY_WIDTH),
             jnp.tile(gla_norm_g[l], GC_HEADS).reshape(1, GC_WIDTH)],
            gla_consts, b, s, GC_KEY_WIDTH, GC_WIDTH, GC_VAL_DIM, [(2 * GC_KEY_WIDTH, F32)])
        weights = [w.astype(BF16) for w in (w_up_a[l], w_up_b[l], w_up_c[l], w_out[l], mlp_w1[l], mlp_w2[l])]
        x = _post_mix_mlp(x, o_a, o_b, o_c, pg, gt_m, sc_f, sh_f, gt_f, norm_mlp_g[l], weights,
                          final_norm_g, l == DEPTH - 1, tm)
    return x
```

```python
import functools
import math

import numpy as np
import jax
import jax.numpy as jnp
from jax import lax
from jax.experimental import pallas as pl
from jax.experimental.pallas import tpu as pltpu

F32 = jnp.float32
BF16 = jnp.bfloat16

D_MODEL = 1024
DEPTH = 4
DA_HEADS = 4
DA_HEAD_DIM = 64
DA_V_DIM = 2 * DA_HEAD_DIM
DA_QK_WIDTH = DA_HEADS * 2 * DA_HEAD_DIM
DA_WIDTH = DA_HEADS * DA_V_DIM
HB_HEADS = 4
HB_HEAD_DIM = 64
HB_WIDTH = HB_HEADS * HB_HEAD_DIM
MAX_NEG_LOGIT = 80.0
GC_HEADS = 4
GC_KEY_DIM = 32
GC_VAL_DIM = 64
GC_KEY_WIDTH = GC_HEADS * GC_KEY_DIM
GC_WIDTH = GC_HEADS * GC_VAL_DIM
GC_GATE_RANK = 16
GC_GATE_NORMALIZER = 16.0
N_BRANCH = 3
D_FF = 4 * D_MODEL
N_MOD = 6
EPS = 1e-6
LOG2_E = 1.4426950408889634
ATTN_SUB_ROWS = 256

LANES = 128
SCAN_CHUNK = 64
SCAN_CHUNKS_PER_STEP = 8
HGRN_MIN_VPU_BLOCK = 4
GLA_MIN_VPU_BLOCK = 2 * SCAN_CHUNK
V7X_VMEM_BYTES = 64 * 1024 * 1024
VMEM_LIMIT = V7X_VMEM_BYTES - 8 * 1024 * 1024


def _dot(a, b):
    return jnp.dot(a, b, preferred_element_type=F32)


def _dot_nt(a, b):
    return lax.dot_general(a, b, (((1,), (1,)), ((), ())), preferred_element_type=F32)


def _split_bf16(a):
    hi = a.astype(BF16)
    lo = (a - hi.astype(F32)).astype(BF16)
    return hi, lo


def _sigmoid(a):
    return 1.0 / (1.0 + jnp.exp(-a))


def _silu(a):
    return a * _sigmoid(a)


def _log_sigmoid(z):
    return -(jnp.maximum(-z, 0.0) + jnp.log(1.0 + jnp.exp(-jnp.abs(z))))


def _params(*sem):
    return pltpu.CompilerParams(dimension_semantics=sem, vmem_limit_bytes=VMEM_LIMIT)


def _const_spec(shape):
    n = len(shape)
    return pl.BlockSpec(shape, lambda *_: (0,) * n, pipeline_mode=pl.Buffered(1))


def _mod_kernel(c_ref, w_ref, b_ref, o_ref):
    cond = _silu(c_ref[...])
    o_ref[0] = jnp.dot(cond, w_ref[0], preferred_element_type=F32,
                       precision=lax.Precision.HIGHEST) + b_ref[0]


def _modulation(c, ada_w, ada_b):
    depth, d, n = ada_w.shape
    b = c.shape[0]
    tn = 1536
    return pl.pallas_call(
        _mod_kernel,
        grid=(depth, n // tn),
        in_specs=[
            pl.BlockSpec((b, d), lambda l, j: (0, 0)),
            pl.BlockSpec((1, d, tn), lambda l, j: (l, 0, j)),
            pl.BlockSpec((1, 1, tn), lambda l, j: (l, 0, j)),
        ],
        out_specs=pl.BlockSpec((1, b, tn), lambda l, j: (l, 0, j)),
        out_shape=jax.ShapeDtypeStruct((depth, b, n), F32),
        compiler_params=_params("arbitrary", "arbitrary"),
        name="modulation",
    )(c, ada_w, ada_b.reshape(depth, 1, n))


def _lb_kernel(z_ref, o_ref):
    z = z_ref[...]
    e = jnp.exp(z - jnp.max(z, axis=0, keepdims=True))
    w = e / jnp.sum(e, axis=0, keepdims=True)
    run = jnp.zeros_like(w[0:1])
    for l in range(z.shape[0]):
        run = run + w[l:l + 1]
        o_ref[l:l + 1, :] = run - w[0:1]


def _lower_bounds(logits):
    depth = logits.shape[0]
    flat = logits.reshape(depth, -1).astype(F32)
    out = pl.pallas_call(
        _lb_kernel,
        out_shape=jax.ShapeDtypeStruct(flat.shape, F32),
        name="hgrn_lower_bounds",
    )(flat)
    return out.reshape(logits.shape)


PROJ_COL_CHUNK = 512


def _pre_kernel(x_ref, sc_ref, sh_ref, g_ref, *refs):
    n = len(refs) // 2
    w_refs, o_refs = refs[:n], refs[n:]
    x = x_ref[0]
    ms = jnp.mean(x * x, axis=-1, keepdims=True)
    y = x * lax.rsqrt(ms + EPS) * g_ref[...]
    h = (y * (1.0 + sc_ref[0]) + sh_ref[0]).astype(BF16)
    for w_ref, o_ref in zip(w_refs, o_refs):
        width = w_ref.shape[1]
        for c0 in range(0, width, PROJ_COL_CHUNK):
            c1 = min(c0 + PROJ_COL_CHUNK, width)
            o_ref[0, :, c0:c1] = _dot(h, w_ref[:, c0:c1]).astype(o_ref.dtype)


def _pre_project(x, sc, sh, g, weights, out_dtypes, tm):
    b, s, d = x.shape
    in_specs = [
        pl.BlockSpec((1, tm, d), lambda i, j: (i, j, 0)),
        pl.BlockSpec((1, 1, d), lambda i, j: (i, 0, 0)),
        pl.BlockSpec((1, 1, d), lambda i, j: (i, 0, 0)),
        _const_spec((1, d)),
    ] + [_const_spec(w.shape) for w in weights]
    out_specs = [pl.BlockSpec((1, tm, w.shape[1]), lambda i, j: (i, j, 0)) for w in weights]
    out_shape = [jax.ShapeDtypeStruct((b, s, w.shape[1]), dt) for w, dt in zip(weights, out_dtypes)]
    return pl.pallas_call(
        _pre_kernel,
        grid=(b, s // tm),
        in_specs=in_specs,
        out_specs=out_specs,
        out_shape=out_shape,
        compiler_params=_params("parallel", "arbitrary"),
        name="prenorm_project",
    )(x, sc, sh, g.reshape(1, d), *weights)


def _attn_kernel(q_ref, k_ref, v_ref, slope_ref, lam_ref, g_ref, o_ref, bias_ref, *, lam_init, tq):
    qi = pl.program_id(0)

    @pl.when(pl.program_id(2) == 0)
    def _():
        s = bias_ref.shape[1]
        row = lax.broadcasted_iota(jnp.int32, (tq, s), 0) + qi * tq
        col = lax.broadcasted_iota(jnp.int32, (tq, s), 1)
        dist = jnp.abs(row - col).astype(F32)
        bias_ref[...] = -(slope_ref[0][:, 0:1] * LOG2_E) * dist

    first = lax.broadcasted_iota(jnp.int32, (1, 2 * DA_HEAD_DIM), 1) < DA_HEAD_DIM
    k = k_ref[0]
    v = v_ref[0]
    sub = min(ATTN_SUB_ROWS, tq)

    def scores(r0):
        qs = (q_ref[0, r0:r0 + sub, :].astype(F32) * (DA_HEAD_DIM ** -0.5 * LOG2_E)).astype(BF16)
        zero = jnp.zeros_like(qs)
        return _dot_nt(jnp.where(first, qs, zero), k), _dot_nt(jnp.where(first, zero, qs), k)

    v_ones = jnp.concatenate([v, jnp.ones_like(v)], axis=1)

    def softmax_v(sc, r0):
        m = jnp.max(sc + bias_ref[r0:r0 + sub, :], axis=-1, keepdims=True)
        pv = _dot(jnp.exp2((sc - m) + bias_ref[r0:r0 + sub, :]).astype(BF16), v_ones)
        return pv[:, :DA_V_DIM] / pv[:, DA_V_DIM:DA_V_DIM + 1]

    lp = lam_ref[...]
    lam = (jnp.exp(jnp.sum(lp[0:1] * lp[1:2], axis=-1, keepdims=True))
           - jnp.exp(jnp.sum(lp[2:3] * lp[3:4], axis=-1, keepdims=True)) + lam_init)
    nxt = scores(0)
    for r0 in range(0, tq, sub):
        sc1, sc2 = nxt
        if r0 + sub < tq:
            nxt = scores(r0 + sub)
        o = softmax_v(sc1, r0) - lam * softmax_v(sc2, r0)
        y = o * lax.rsqrt(jnp.mean(o * o, axis=-1, keepdims=True) + EPS) * g_ref[...]
        o_ref[0, r0:r0 + sub, :] = (y * (1.0 - lam_init)).astype(o_ref.dtype)


def _diff_attention(pa, lam_params, subln_g, layer_idx, tq):
    b, s, _ = pa.shape
    h = DA_HEADS
    w = DA_V_DIM
    lam_init = 0.8 - 0.6 * math.exp(-0.3 * layer_idx)
    slopes = np.array([2.0 ** (-8.0 * (i + 1) / h) for i in range(h)], np.float32)
    slopes = jnp.asarray(np.broadcast_to(slopes[:, None, None], (h, 1, LANES)).copy())
    return pl.pallas_call(
        functools.partial(_attn_kernel, lam_init=lam_init, tq=tq),
        grid=(s // tq, h, b),
        in_specs=[
            pl.BlockSpec((1, tq, w), lambda qi, hh, bb: (bb, qi, hh)),
            pl.BlockSpec((1, s, w), lambda qi, hh, bb: (bb, 0, h + hh)),
            pl.BlockSpec((1, s, w), lambda qi, hh, bb: (bb, 0, 2 * h + hh)),
            pl.BlockSpec((1, 1, LANES), lambda qi, hh, bb: (hh, 0, 0)),
            pl.BlockSpec(lam_params.shape, lambda qi, hh, bb: (0, 0)),
            pl.BlockSpec((1, w), lambda qi, hh, bb: (0, 0)),
        ],
        out_specs=pl.BlockSpec((1, tq, w), lambda qi, hh, bb: (bb, qi, hh)),
        out_shape=jax.ShapeDtypeStruct((b, s, h * w), BF16),
        scratch_shapes=[pltpu.VMEM((tq, s), F32)],
        compiler_params=_params("arbitrary", "arbitrary", "arbitrary"),
        name="diff_attention",
    )(pa, pa, pa, slopes, lam_params.astype(F32), subln_g.reshape(1, w).astype(F32))


def _scan_constants(c, h, dk, dv, min_vpu_block):
    n_lev = int(math.log2(c))
    assert 2 ** n_lev == c
    assert dv & (dv - 1) == 0, "1/dv must be exact in bf16"
    t = np.arange(c)[:, None]
    r = np.arange(c)[None, :]
    blocks = [(r <= t)]
    if c < min_vpu_block:
        blocks.append(r > t)
    lev = np.full((c, c), -1, np.int32)
    lev[np.arange(c), np.arange(c)] = n_lev
    for li in range(n_lev):
        bs = c >> (li + 1)
        tb, rb = t // bs, r // bs
        if bs < min_vpu_block:
            blocks.append((rb == tb) & np.where(tb % 2 == 1, r <= t, r > t))
        lev[(tb % 2 == 1) & (rb == tb - 1)] = li
    flip = lambda m: m[::-1, ::-1]
    twice = lambda ms: np.tile(np.concatenate(ms, axis=0).astype(np.float32), (1, 2))
    w_f = twice(blocks)
    w_b = twice([flip(m) for m in blocks])
    bdn = ((np.arange(h * dv) // dv)[:, None] == (np.arange(h * dv) // dv)[None, :]).astype(np.float32) / dv
    return dict(
        w_f=jnp.asarray(w_f, BF16), w_b=jnp.asarray(w_b, BF16),
        lev_f=jnp.asarray(np.tile(lev, (1, 2))), lev_b=jnp.asarray(np.tile(flip(lev), (1, 2))),
        bdn=jnp.asarray(bdn, BF16),
    )


SCAN_CONST_NAMES = ("w_f", "w_b", "lev_f", "lev_b", "bdn")


def _scan_chunks(chains, n_heads, min_vpu_block):
    c = chains[0][0].shape[0]
    n_lev = int(math.log2(c))
    hv = chains[0][2].shape[1]
    dv = hv // n_heads
    assert 2 * c == LANES and 2 * dv == LANES and n_heads % 2 == 0
    blk = lambda ex, i: ex[i * c:(i + 1) * c]
    lane = lax.broadcasted_iota(jnp.int32, (1, LANES), 1)
    half = [jnp.where(lane < LANES // 2, 1.0, 0.0).astype(BF16), jnp.where(lane < LANES // 2, 0.0, 1.0).astype(BF16)]

    def head_stack(a):
        r, w = a.shape
        wh = w // n_heads
        per_tile = LANES // wh
        rows = []
        for h in range(n_heads):
            keep = jnp.where(lane // wh == h % per_tile, 1.0, 0.0).astype(BF16)
            rows.append(jnp.concatenate(
                [a[:, t * LANES:(t + 1) * LANES] * keep if t == h // per_tile else jnp.zeros((r, LANES), BF16)
                 for t in range(w // LANES)], axis=1))
        return jnp.concatenate(rows, axis=0)

    def value_rows(v):
        vt = jnp.concatenate([v, v], axis=0).T
        return jnp.concatenate(
            [jnp.where(lane < LANES // 2, vt[2 * g * dv:(2 * g + 1) * dv], vt[(2 * g + 1) * dv:(2 * g + 2) * dv])
             for g in range(n_heads // 2)], axis=1)

    def scores(qa, ka):
        kt = jnp.concatenate([ka, ka], axis=0).T
        dk = kt.shape[0] // n_heads
        cols = []
        for g in range(n_heads // 2):
            cols.append(jnp.concatenate(
                [kt[h * dk:(h + 1) * dk] * half[h % 2] if h // 2 == g else jnp.zeros((dk, LANES), BF16)
                 for h in range(n_heads)], axis=0))
        return _dot(qa, jnp.concatenate(cols, axis=1))

    sums = [None] * len(chains)
    for first in range(len(chains)):
        w_ref = chains[first][5]
        if sums[first] is not None:
            continue
        idx = [i for i, ch in enumerate(chains) if ch[5] is w_ref]
        parts = [_split_bf16(chains[i][3]) for i in idx]
        rhs = jnp.concatenate([jnp.concatenate([hi, lo], axis=0) for hi, lo in parts], axis=1)
        total = _dot(w_ref[...], rhs)
        width = chains[idx[0]][3].shape[1]
        for n, i in enumerate(idx):
            sums[i] = total[:, n * width:(n + 1) * width]

    def decay_factors(ch, sm):
        reverse = ch[7]
        cum = blk(sm, 0)
        hk = cum.shape[1]

        def against(ref_row_of_group, group):
            ref = jnp.concatenate([jnp.broadcast_to(cum[r:r + 1], (group, hk))
                                   for r in (ref_row_of_group(j) for j in range(c // group))], axis=0)
            return jnp.exp2(-jnp.abs(cum - ref))

        last = 0 if reverse else c - 1
        small = 1
        if c >= min_vpu_block:
            update = against(lambda j: last, c)
        else:
            update = jnp.exp2(blk(sm, small))
            small += 1
        factors = [jnp.exp2(cum), update]
        for li in range(n_lev):
            bs = c >> (li + 1)
            if bs >= min_vpu_block:
                factors.append(against(lambda j: 2 * bs * j + (bs if reverse else bs - 1), 2 * bs))
            else:
                factors.append(jnp.exp2(blk(sm, small)))
                small += 1
        return factors, jnp.exp2(cum[last:last + 1])

    exs, e_tots = zip(*[decay_factors(ch, sm) for ch, sm in zip(chains, sums)])

    def owned(ch, li):
        own = ch[6][...] == li
        return jnp.concatenate([own] * (n_heads // 2), axis=1)

    accs = [jnp.where(owned(ch, n_lev), scores(ch[0], ch[1]), 0.0) for ch in chains]
    for li in range(n_lev):
        for i, (ch, ex) in enumerate(zip(chains, exs)):
            e = ex[2 + li].astype(BF16)
            accs[i] = jnp.where(owned(ch, li), scores(ch[0] * e, ch[1] * e), accs[i])
    outs = [_dot(acc.astype(BF16), head_stack(ch[2])) for ch, acc in zip(chains, accs)]
    news = [_dot(value_rows(ch[2]), head_stack(ch[1] * ex[1].astype(BF16)))
            for ch, ex in zip(chains, exs)]
    for i, (ch, ex) in enumerate(zip(chains, exs)):
        st_ref = ch[4]
        st = st_ref[...]
        outs[i] = outs[i] + _dot_nt(ch[0] * ex[0].astype(BF16), head_stack(st.astype(BF16)))
        st_ref[...] = st * e_tots[i] + news[i]
    return outs


def _head_rmsnorm_gate(o, gain, gate, bdn):
    ms = _dot((o * o).astype(BF16), bdn)
    return o * lax.rsqrt(ms + EPS) * gain * _silu(gate)


def _bidir_scan(prologue, prep_f, prep_b, gate_fn, gain_ref, consts, o_ref, of_ref, ob_ref, stf_ref,
                stb_ref, n_heads, min_vpu_block):
    w_f, w_b, lev_f, lev_b, bdn = consts
    s = of_ref.shape[0]
    c = SCAN_CHUNK
    group = min(SCAN_CHUNKS_PER_STEP, s // c)
    block = group * c
    assert s % block == 0
    nb = s // block
    stf_ref[...] = jnp.zeros_like(stf_ref)
    stb_ref[...] = jnp.zeros_like(stb_ref)

    def prologue_step(jb, carry):
        prologue(pl.multiple_of(jb * block, block), block)
        return carry

    lax.fori_loop(0, nb, prologue_step, 0, unroll=True)

    def step(j, carry):
        rows, chains = [], []
        for g in range(group):
            rf = pl.multiple_of(j * block + g * c, c)
            rb = pl.multiple_of((nb - 1 - j) * block + (group - 1 - g) * c, c)
            rows += [(of_ref, rf), (ob_ref, rb)]
            chains += [prep_f(rf) + (stf_ref, w_f, lev_f, False), prep_b(rb) + (stb_ref, w_b, lev_b, True)]
        outs = _scan_chunks(chains, n_heads, min_vpu_block)
        for (ref, r0), o in zip(rows, outs):
            ref[pl.ds(r0, c), :] = o
        return carry

    lax.fori_loop(0, nb, step, 0)
    for jb in range(nb):
        rows = pl.ds(jb * block, block)
        y = _head_rmsnorm_gate(of_ref[rows, :] + ob_ref[rows, :], gain_ref[...],
                               gate_fn(jb * block, block), bdn[...])
        o_ref[0, rows, :] = y.astype(o_ref.dtype)


def _hgrn_kernel(pb_ref, pf_ref, lb_ref, gain_ref, *refs):
    consts = refs[:len(SCAN_CONST_NAMES)]
    (o_ref, of_ref, ob_ref, stf_ref, stb_ref, q_ref, kf_ref, kb_ref, laf_ref,
     lab_ref) = refs[len(SCAN_CONST_NAMES):]
    k_refs = (kf_ref, kb_ref)
    la_refs = (laf_ref, lab_ref)
    c = SCAN_CHUNK
    wd = HB_WIDTH

    def prologue(r0, rows):
        q_ref[pl.ds(r0, rows), :] = (_silu(pb_ref[0, pl.ds(r0, rows), 0:wd].astype(F32))
                                     * HB_HEAD_DIM ** -0.5).astype(q_ref.dtype)
        for direction in range(2):
            lb = lb_ref[direction:direction + 1, :]
            z = pf_ref[0, pl.ds(r0, rows), direction * wd:(direction + 1) * wd]
            nz2 = z * -LOG2_E
            e = jnp.exp2(-jnp.abs(nz2))
            r = 1.0 / (1.0 + e)
            cap = jnp.exp2(jnp.minimum(nz2, MAX_NEG_LOGIT * LOG2_E))
            la_refs[direction][pl.ds(r0, rows), :] = (
                jnp.log((1.0 + lb * cap) * r) * LOG2_E - jnp.maximum(nz2, 0.0))
            k_refs[direction][pl.ds(r0, rows), :] = (
                (1.0 - lb) * jnp.where(z >= 0.0, e * r, r)).astype(k_refs[direction].dtype)

    def prep(direction):
        def fn(r0):
            rows = pl.ds(r0, c)
            return (q_ref[rows, :], k_refs[direction][rows, :], pb_ref[0, rows, wd:2 * wd],
                    la_refs[direction][rows, :])
        return fn

    gate_fn = lambda r0, rows: pb_ref[0, pl.ds(r0, rows), 2 * wd:3 * wd].astype(F32)
    _bidir_scan(prologue, prep(0), prep(1), gate_fn, gain_ref, consts, o_ref, of_ref, ob_ref, stf_ref,
                stb_ref, HB_HEADS, HGRN_MIN_VPU_BLOCK)


def _gla_kernel(pc_ref, plr_ref, w2h_ref, w2l_ref, gb_ref, gain_ref, *refs):
    consts = refs[:len(SCAN_CONST_NAMES)]
    o_ref, of_ref, ob_ref, stf_ref, stb_ref, la_ref = refs[len(SCAN_CONST_NAMES):]
    c = SCAN_CHUNK
    kw = GC_KEY_WIDTH
    vw = GC_WIDTH

    def prologue(r0, rows):
        lr_hi, lr_lo = _split_bf16(plr_ref[0, pl.ds(r0, rows), :])
        w2h = w2h_ref[...]
        z = _dot(lr_hi, w2h) + _dot(lr_lo, w2h) + _dot(lr_hi, w2l_ref[...]) + gb_ref[...]
        la_ref[pl.ds(r0, rows), :] = _log_sigmoid(z) * (LOG2_E / GC_GATE_NORMALIZER)

    def prep(direction):
        def fn(r0):
            rows = pl.ds(r0, c)
            qkv = pc_ref[0, rows, :]
            q = (qkv[:, 0:kw].astype(F32) * GC_KEY_DIM ** -0.5).astype(BF16)
            k = qkv[:, kw:2 * kw]
            v = qkv[:, 2 * kw:2 * kw + vw]
            return q, k, v, la_ref[rows, direction * kw:(direction + 1) * kw]
        return fn

    gate_fn = lambda r0, rows: pc_ref[0, pl.ds(r0, rows), 2 * kw + vw:2 * kw + 2 * vw].astype(F32)
    _bidir_scan(prologue, prep(0), prep(1), gate_fn, gain_ref, consts, o_ref, of_ref, ob_ref, stf_ref,
                stb_ref, GC_HEADS, GLA_MIN_VPU_BLOCK)


def _scan_call(kernel, name, seq_inputs, small_inputs, consts, b, s, hk, hv, dv, token_scratch):
    const_arrays = [consts[n] for n in SCAN_CONST_NAMES]
    in_specs = ([pl.BlockSpec((1, s, a.shape[2]), lambda i: (i, 0, 0)) for a in seq_inputs]
                + [_const_spec(a.shape) for a in small_inputs]
                + [_const_spec(a.shape) for a in const_arrays])
    return pl.pallas_call(
        kernel,
        grid=(b,),
        in_specs=in_specs,
        out_specs=pl.BlockSpec((1, s, hv), lambda i: (i, 0, 0)),
        out_shape=jax.ShapeDtypeStruct((b, s, hv), BF16),
        scratch_shapes=[pltpu.VMEM((s, hv), F32), pltpu.VMEM((s, hv), F32),
                        pltpu.VMEM((dv, hk), F32), pltpu.VMEM((dv, hk), F32)]
        + [pltpu.VMEM((s, w), dt) for w, dt in token_scratch],
        compiler_params=_params("parallel"),
        name=name,
    )(*seq_inputs, *small_inputs, *const_arrays)


FF_CHUNK = 1024


def _post_kernel(x_ref, oa_ref, ob_ref, oc_ref, pg_ref, gtm_ref, scf_ref, shf_ref, gtf_ref, g2_ref,
                 wua_ref, wub_ref, wuc_ref, wo_ref, w1_ref, w2_ref, fg_ref, out_ref, *, final):
    d = x_ref.shape[2]
    gate = lambda i: _sigmoid(pg_ref[0, :, i * d:(i + 1) * d].astype(F32))
    merged = (gate(0) * _dot(oa_ref[0], wua_ref[...])
              + gate(1) * _dot(ob_ref[0], wub_ref[...])
              + gate(2) * _dot(oc_ref[0], wuc_ref[...]))
    x1 = x_ref[0] + gtm_ref[0] * _dot(merged.astype(BF16), wo_ref[...])
    ms = jnp.mean(x1 * x1, axis=-1, keepdims=True)
    y = x1 * lax.rsqrt(ms + EPS) * g2_ref[...]
    h = (y * (1.0 + scf_ref[0]) + shf_ref[0]).astype(BF16)
    acc = jnp.zeros_like(x1)
    for c0 in range(0, w1_ref.shape[1], FF_CHUNK):
        u = jnp.maximum(_dot(h, w1_ref[:, c0:c0 + FF_CHUNK]), 0.0)
        acc = acc + _dot((u * u).astype(BF16), w2_ref[c0:c0 + FF_CHUNK, :])
    x2 = x1 + gtf_ref[0] * acc
    if final:
        x2 = x2 * lax.rsqrt(jnp.mean(x2 * x2, axis=-1, keepdims=True) + EPS) * fg_ref[...]
    out_ref[0] = x2


def _post_mix_mlp(x, oa, ob, oc, pg, gtm, scf, shf, gtf, g2, weights, fg, final, tm):
    b, s, d = x.shape
    tok = lambda a: pl.BlockSpec((1, tm, a.shape[2]), lambda i, j: (i, j, 0))
    vec = pl.BlockSpec((1, 1, d), lambda i, j: (i, 0, 0))
    return pl.pallas_call(
        functools.partial(_post_kernel, final=final),
        grid=(b, s // tm),
        in_specs=[tok(x), tok(oa), tok(ob), tok(oc), tok(pg), vec, vec, vec, vec, _const_spec((1, d))]
        + [_const_spec(w.shape) for w in weights] + [_const_spec((1, d))],
        out_specs=pl.BlockSpec((1, tm, d), lambda i, j: (i, j, 0)),
        out_shape=jax.ShapeDtypeStruct((b, s, d), F32),
        compiler_params=_params("parallel", "arbitrary"),
        name="merge_out_mlp",
    )(x, oa, ob, oc, pg, gtm, scf, shf, gtf, g2.reshape(1, d), *weights, fg.reshape(1, d))


def _input_projection_groups(w):
    o_b = 2 * DA_QK_WIDTH + DA_WIDTH
    o_c = o_b + 5 * HB_WIDTH
    o_lr = o_c + 2 * GC_KEY_WIDTH + 2 * GC_WIDTH
    o_g = o_lr + 2 * GC_GATE_RANK
    wb = HB_WIDTH
    attn = w[:, :o_b]
    hgrn_qig = jnp.concatenate([w[:, o_b:o_b + wb], w[:, o_b + 3 * wb:o_b + 5 * wb]], axis=1)
    hgrn_f = w[:, o_b + wb:o_b + 3 * wb]
    gla = w[:, o_c:o_lr]
    lr = jnp.pad(w[:, o_lr:o_g], ((0, 0), (0, LANES - 2 * GC_GATE_RANK)))
    gates = w[:, o_g:]
    return [g.astype(BF16) for g in (attn, hgrn_qig, hgrn_f, gla, lr, gates)]


PROJ_OUT_DTYPES = (BF16, BF16, F32, BF16, F32, BF16)


def kernel(x, c, ada_w, ada_b, norm_mix_g, norm_mlp_g, w_in, diff_lambda, diff_subln_g, hgrn_lb_logits,
           hgrn_norm_g, gla_gate_w2, gla_gate_b, gla_norm_g, w_up_a, w_up_b, w_up_c, w_out, mlp_w1, mlp_w2,
           final_norm_g):
    b, s, d = x.shape
    tm = min(512, s)
    tq = min(2048, s)
    mod = _modulation(c, ada_w, ada_b).reshape(DEPTH, b, N_MOD, 1, d)
    lower_bounds = _lower_bounds(hgrn_lb_logits)
    hgrn_consts = _scan_constants(SCAN_CHUNK, HB_HEADS, HB_HEAD_DIM, HB_HEAD_DIM, HGRN_MIN_VPU_BLOCK)
    gla_consts = _scan_constants(SCAN_CHUNK, GC_HEADS, GC_KEY_DIM, GC_VAL_DIM, GLA_MIN_VPU_BLOCK)
    for l in range(DEPTH):
        sh_m, sc_m, gt_m, sh_f, sc_f, gt_f = [mod[l, :, i] for i in range(N_MOD)]
        pa, pb, pf, pc, plr, pg = _pre_project(
            x, sc_m, sh_m, norm_mix_g[l], _input_projection_groups(w_in[l]), PROJ_OUT_DTYPES, tm)
        o_a = _diff_attention(pa, diff_lambda[l], diff_subln_g[l], l, tq)
        o_b = _scan_call(
            _hgrn_kernel, "hgrn2_scan", [pb, pf],
            [lower_bounds[l], jnp.tile(hgrn_norm_g[l], HB_HEADS).reshape(1, HB_WIDTH)],
            hgrn_consts, b, s, HB_WIDTH, HB_WIDTH, HB_HEAD_DIM,
            [(HB_WIDTH, BF16)] * 3 + [(HB_WIDTH, F32)] * 2)
        w2 = jnp.zeros((LANES, 2 * GC_KEY_WIDTH), F32)
        w2 = w2.at[0:GC_GATE_RANK, 0:GC_KEY_WIDTH].set(gla_gate_w2[l, 0])
        w2 = w2.at[GC_GATE_RANK:2 * GC_GATE_RANK, GC_KEY_WIDTH:].set(gla_gate_w2[l, 1])
        w2_hi = w2.astype(BF16)
        w2_lo = (w2 - w2_hi.astype(F32)).astype(BF16)
        o_c = _scan_call(
            _gla_kernel, "gla_scan", [pc, plr],
            [w2_hi, w2_lo, gla_gate_b[l].reshape(1, 2 * GC_KEY_WIDTH),
             jnp.tile(gla_norm_g[l], GC_HEADS).reshape(1, GC_WIDTH)],
            gla_consts, b, s, GC_KEY_WIDTH, GC_WIDTH, GC_VAL_DIM, [(2 * GC_KEY_WIDTH, F32)])
        weights = [w.astype(BF16) for w in (w_up_a[l], w_up_b[l], w_up_c[l], w_out[l], mlp_w1[l], mlp_w2[l])]
        x = _post_mix_mlp(x, o_a, o_b, o_c, pg, gt_m, sc_f, sh_f, gt_f, norm_mlp_g[l], weights,
                          final_norm_g, l == DEPTH - 1, tm)
    return x
```

```python
import functools
import math

import numpy as np
import jax
import jax.numpy as jnp
from jax import lax
from jax.experimental import pallas as pl
from jax.experimental.pallas import tpu as pltpu

F32 = jnp.float32
BF16 = jnp.bfloat16

D_MODEL = 1024
DEPTH = 4
DA_HEADS = 4
DA_HEAD_DIM = 64
DA_V_DIM = 2 * DA_HEAD_DIM
DA_QK_WIDTH = DA_HEADS * 2 * DA_HEAD_DIM
DA_WIDTH = DA_HEADS * DA_V_DIM
HB_HEADS = 4
HB_HEAD_DIM = 64
HB_WIDTH = HB_HEADS * HB_HEAD_DIM
MAX_NEG_LOGIT = 80.0
GC_HEADS = 4
GC_KEY_DIM = 32
GC_VAL_DIM = 64
GC_KEY_WIDTH = GC_HEADS * GC_KEY_DIM
GC_WIDTH = GC_HEADS * GC_VAL_DIM
GC_GATE_RANK = 16
GC_GATE_NORMALIZER = 16.0
N_BRANCH = 3
D_FF = 4 * D_MODEL
N_MOD = 6
EPS = 1e-6
LOG2_E = 1.4426950408889634
ATTN_SUB_ROWS = 256

LANES = 128
SCAN_CHUNK = 64
SCAN_CHUNKS_PER_STEP = 8
HGRN_MIN_VPU_BLOCK = 4
GLA_MIN_VPU_BLOCK = 2 * SCAN_CHUNK
V7X_VMEM_BYTES = 64 * 1024 * 1024
VMEM_LIMIT = V7X_VMEM_BYTES - 8 * 1024 * 1024


def _dot(a, b):
    return jnp.dot(a, b, preferred_element_type=F32)


def _dot_nt(a, b):
    return lax.dot_general(a, b, (((1,), (1,)), ((), ())), preferred_element_type=F32)


def _split_bf16(a):
    hi = a.astype(BF16)
    lo = (a - hi.astype(F32)).astype(BF16)
    return hi, lo


def _sigmoid(a):
    return 1.0 / (1.0 + jnp.exp(-a))


def _silu(a):
    return a * _sigmoid(a)


def _log_sigmoid(z):
    return -(jnp.maximum(-z, 0.0) + jnp.log(1.0 + jnp.exp(-jnp.abs(z))))


def _params(*sem):
    return pltpu.CompilerParams(dimension_semantics=sem, vmem_limit_bytes=VMEM_LIMIT)


def _const_spec(shape):
    n = len(shape)
    return pl.BlockSpec(shape, lambda *_: (0,) * n, pipeline_mode=pl.Buffered(1))


def _mod_kernel(c_ref, w_ref, b_ref, o_ref):
    cond = _silu(c_ref[...])
    o_ref[0] = jnp.dot(cond, w_ref[0], preferred_element_type=F32,
                       precision=lax.Precision.HIGHEST) + b_ref[0]


def _modulation(c, ada_w, ada_b):
    depth, d, n = ada_w.shape
    b = c.shape[0]
    tn = 1536
    return pl.pallas_call(
        _mod_kernel,
        grid=(depth, n // tn),
        in_specs=[
            pl.BlockSpec((b, d), lambda l, j: (0, 0)),
            pl.BlockSpec((1, d, tn), lambda l, j: (l, 0, j)),
            pl.BlockSpec((1, 1, tn), lambda l, j: (l, 0, j)),
        ],
        out_specs=pl.BlockSpec((1, b, tn), lambda l, j: (l, 0, j)),
        out_shape=jax.ShapeDtypeStruct((depth, b, n), F32),
        compiler_params=_params("arbitrary", "arbitrary"),
        name="modulation",
    )(c, ada_w, ada_b.reshape(depth, 1, n))


def _lb_kernel(z_ref, o_ref):
    z = z_ref[...]
    e = jnp.exp(z - jnp.max(z, axis=0, keepdims=True))
    w = e / jnp.sum(e, axis=0, keepdims=True)
    run = jnp.zeros_like(w[0:1])
    for l in range(z.shape[0]):
        run = run + w[l:l + 1]
        o_ref[l:l + 1, :] = run - w[0:1]


def _lower_bounds(logits):
    depth = logits.shape[0]
    flat = logits.reshape(depth, -1).astype(F32)
    out = pl.pallas_call(
        _lb_kernel,
        out_shape=jax.ShapeDtypeStruct(flat.shape, F32),
        name="hgrn_lower_bounds",
    )(flat)
    return out.reshape(logits.shape)


PROJ_COL_CHUNK = 512


def _pre_kernel(x_ref, sc_ref, sh_ref, g_ref, *refs):
    n = len(refs) // 2
    w_refs, o_refs = refs[:n], refs[n:]
    x = x_ref[0]
    ms = jnp.mean(x * x, axis=-1, keepdims=True)
    y = x * lax.rsqrt(ms + EPS) * g_ref[...]
    h = (y * (1.0 + sc_ref[0]) + sh_ref[0]).astype(BF16)
    for w_ref, o_ref in zip(w_refs, o_refs):
        width = w_ref.shape[1]
        for c0 in range(0, width, PROJ_COL_CHUNK):
            c1 = min(c0 + PROJ_COL_CHUNK, width)
            o_ref[0, :, c0:c1] = _dot(h, w_ref[:, c0:c1]).astype(o_ref.dtype)


def _pre_project(x, sc, sh, g, weights, out_dtypes, tm):
    b, s, d = x.shape
    in_specs = [
        pl.BlockSpec((1, tm, d), lambda i, j: (i, j, 0)),
        pl.BlockSpec((1, 1, d), lambda i, j: (i, 0, 0)),
        pl.BlockSpec((1, 1, d), lambda i, j: (i, 0, 0)),
        _const_spec((1, d)),
    ] + [_const_spec(w.shape) for w in weights]
    out_specs = [pl.BlockSpec((1, tm, w.shape[1]), lambda i, j: (i, j, 0)) for w in weights]
    out_shape = [jax.ShapeDtypeStruct((b, s, w.shape[1]), dt) for w, dt in zip(weights, out_dtypes)]
    return pl.pallas_call(
        _pre_kernel,
        grid=(b, s // tm),
        in_specs=in_specs,
        out_specs=out_specs,
        out_shape=out_shape,
        compiler_params=_params("parallel", "arbitrary"),
        name="prenorm_project",
    )(x, sc, sh, g.reshape(1, d), *weights)


def _attn_kernel(q_ref, k_ref, v_ref, slope_ref, lam_ref, g_ref, o_ref, bias_ref, *, lam_init, tq):
    qi = pl.program_id(0)

    @pl.when(pl.program_id(2) == 0)
    def _():
        s = bias_ref.shape[1]
        row = lax.broadcasted_iota(jnp.int32, (tq, s), 0) + qi * tq
        col = lax.broadcasted_iota(jnp.int32, (tq, s), 1)
        dist = jnp.abs(row - col).astype(F32)
        bias_ref[...] = -(slope_ref[0][:, 0:1] * LOG2_E) * dist

    first = lax.broadcasted_iota(jnp.int32, (1, 2 * DA_HEAD_DIM), 1) < DA_HEAD_DIM
    k = k_ref[0]
    v = v_ref[0]
    sub = min(ATTN_SUB_ROWS, tq)

    def scores(r0):
        qs = (q_ref[0, r0:r0 + sub, :].astype(F32) * (DA_HEAD_DIM ** -0.5 * LOG2_E)).astype(BF16)
        zero = jnp.zeros_like(qs)
        return _dot_nt(jnp.where(first, qs, zero), k), _dot_nt(jnp.where(first, zero, qs), k)

    v_ones = jnp.concatenate([v, jnp.ones_like(v)], axis=1)

    def softmax_v(sc, r0):
        m = jnp.max(sc + bias_ref[r0:r0 + sub, :], axis=-1, keepdims=True)
        pv = _dot(jnp.exp2((sc - m) + bias_ref[r0:r0 + sub, :]).astype(BF16), v_ones)
        return pv[:, :DA_V_DIM] / pv[:, DA_V_DIM:DA_V_DIM + 1]

    lp = lam_ref[...]
    lam = (jnp.exp(jnp.sum(lp[0:1] * lp[1:2], axis=-1, keepdims=True))
           - jnp.exp(jnp.sum(lp[2:3] * lp[3:4], axis=-1, keepdims=True)) + lam_init)
    nxt = scores(0)
    for r0 in range(0, tq, sub):
        sc1, sc2 = nxt
        if r0 + sub < tq:
            nxt = scores(r0 + sub)
        o = softmax_v(sc1, r0) - lam * softmax_v(sc2, r0)
        y = o * lax.rsqrt(jnp.mean(o * o, axis=-1, keepdims=True) + EPS) * g_ref[...]
        o_ref[0, r0:r0 + sub, :] = (y * (1.0 - lam_init)).astype(o_ref.dtype)


def _diff_attention(pa, lam_params, subln_g, layer_idx, tq):
    b, s, _ = pa.shape
    h = DA_HEADS
    w = DA_V_DIM
    lam_init = 0.8 - 0.6 * math.exp(-0.3 * layer_idx)
    slopes = np.array([2.0 ** (-8.0 * (i + 1) / h) for i in range(h)], np.float32)
    slopes = jnp.asarray(np.broadcast_to(slopes[:, None, None], (h, 1, LANES)).copy())
    return pl.pallas_call(
        functools.partial(_attn_kernel, lam_init=lam_init, tq=tq),
        grid=(s // tq, h, b),
        in_specs=[
            pl.BlockSpec((1, tq, w), lambda qi, hh, bb: (bb, qi, hh)),
            pl.BlockSpec((1, s, w), lambda qi, hh, bb: (bb, 0, h + hh)),
            pl.BlockSpec((1, s, w), lambda qi, hh, bb: (bb, 0, 2 * h + hh)),
            pl.BlockSpec((1, 1, LANES), lambda qi, hh, bb: (hh, 0, 0)),
            pl.BlockSpec(lam_params.shape, lambda qi, hh, bb: (0, 0)),
            pl.BlockSpec((1, w), lambda qi, hh, bb: (0, 0)),
        ],
        out_specs=pl.BlockSpec((1, tq, w), lambda qi, hh, bb: (bb, qi, hh)),
        out_shape=jax.ShapeDtypeStruct((b, s, h * w), BF16),
        scratch_shapes=[pltpu.VMEM((tq, s), F32)],
        compiler_params=_params("arbitrary", "arbitrary", "arbitrary"),
        name="diff_attention",
    )(pa, pa, pa, slopes, lam_params.astype(F32), subln_g.reshape(1, w).astype(F32))


def _scan_constants(c, h, dk, dv, min_vpu_block):
    n_lev = int(math.log2(c))
    assert 2 ** n_lev == c
    assert dv & (dv - 1) == 0, "1/dv must be exact in bf16"
    t = np.arange(c)[:, None]
    r = np.arange(c)[None, :]
    blocks = [(r <= t)]
    if c < min_vpu_block:
        blocks.append(r > t)
    lev = np.full((c, c), -1, np.int32)
    lev[np.arange(c), np.arange(c)] = n_lev
    for li in range(n_lev):
        bs = c >> (li + 1)
        tb, rb = t // bs, r // bs
        if bs < min_vpu_block:
            blocks.append((rb == tb) & np.where(tb % 2 == 1, r <= t, r > t))
        lev[(tb % 2 == 1) & (rb == tb - 1)] = li
    flip = lambda m: m[::-1, ::-1]
    twice = lambda ms: np.tile(np.concatenate(ms, axis=0).astype(np.float32), (1, 2))
    w_f = twice(blocks)
    w_b = twice([flip(m) for m in blocks])
    bdn = ((np.arange(h * dv) // dv)[:, None] == (np.arange(h * dv) // dv)[None, :]).astype(np.float32) / dv
    return dict(
        w_f=jnp.asarray(w_f, BF16), w_b=jnp.asarray(w_b, BF16),
        lev_f=jnp.asarray(np.tile(lev, (1, 2))), lev_b=jnp.asarray(np.tile(flip(lev), (1, 2))),
        bdn=jnp.asarray(bdn, BF16),
    )


SCAN_CONST_NAMES = ("w_f", "w_b", "lev_f", "lev_b", "bdn")


def _scan_chunks(chains, n_heads, min_vpu_block):
    c = chains[0][0].shape[0]
    n_lev = int(math.log2(c))
    hv = chains[0][2].shape[1]
    dv = hv // n_heads
    assert 2 * c == LANES and 2 * dv == LANES and n_heads % 2 == 0
    blk = lambda ex, i: ex[i * c:(i + 1) * c]
    lane = lax.broadcasted_iota(jnp.int32, (1, LANES), 1)
    half = [jnp.where(lane < LANES // 2, 1.0, 0.0).astype(BF16), jnp.where(lane < LANES // 2, 0.0, 1.0).astype(BF16)]

    def head_stack(a):
        r, w = a.shape
        wh = w // n_heads
        per_tile = LANES // wh
        rows = []
        for h in range(n_heads):
            keep = jnp.where(lane // wh == h % per_tile, 1.0, 0.0).astype(BF16)
            rows.append(jnp.concatenate(
                [a[:, t * LANES:(t + 1) * LANES] * keep if t == h // per_tile else jnp.zeros((r, LANES), BF16)
                 for t in range(w // LANES)], axis=1))
        return jnp.concatenate(rows, axis=0)

    def value_rows(v):
        vt = jnp.concatenate([v, v], axis=0).T
        return jnp.concatenate(
            [jnp.where(lane < LANES // 2, vt[2 * g * dv:(2 * g + 1) * dv], vt[(2 * g + 1) * dv:(2 * g + 2) * dv])
             for g in range(n_heads // 2)], axis=1)

    def scores(qa, ka):
        kt = jnp.concatenate([ka, ka], axis=0).T
        dk = kt.shape[0] // n_heads
        cols = []
        for g in range(n_heads // 2):
            cols.append(jnp.concatenate(
                [kt[h * dk:(h + 1) * dk] * half[h % 2] if h // 2 == g else jnp.zeros((dk, LANES), BF16)
                 for h in range(n_heads)], axis=0))
        return _dot(qa, jnp.concatenate(cols, axis=1))

    sums = [None] * len(chains)
    for first in range(len(chains)):
        w_ref = chains[first][5]
        if sums[first] is not None:
            continue
        idx = [i for i, ch in enumerate(chains) if ch[5] is w_ref]
        parts = [_split_bf16(chains[i][3]) for i in idx]
        rhs = jnp.concatenate([jnp.concatenate([hi, lo], axis=0) for hi, lo in parts], axis=1)
        total = _dot(w_ref[...], rhs)
        width = chains[idx[0]][3].shape[1]
        for n, i in enumerate(idx):
            sums[i] = total[:, n * width:(n + 1) * width]

    def decay_factors(ch, sm):
        reverse = ch[7]
        cum = blk(sm, 0)
        hk = cum.shape[1]

        def against(ref_row_of_group, group):
            ref = jnp.concatenate([jnp.broadcast_to(cum[r:r + 1], (group, hk))
                                   for r in (ref_row_of_group(j) for j in range(c // group))], axis=0)
            return jnp.exp2(-jnp.abs(cum - ref))

        last = 0 if reverse else c - 1
        small = 1
        if c >= min_vpu_block:
            update = against(lambda j: last, c)
        else:
            update = jnp.exp2(blk(sm, small))
            small += 1
        factors = [jnp.exp2(cum), update]
        for li in range(n_lev):
            bs = c >> (li + 1)
            if bs >= min_vpu_block:
                factors.append(against(lambda j: 2 * bs * j + (bs if reverse else bs - 1), 2 * bs))
            else:
                factors.append(jnp.exp2(blk(sm, small)))
                small += 1
        return factors, jnp.exp2(cum[last:last + 1])

    exs, e_tots = zip(*[decay_factors(ch, sm) for ch, sm in zip(chains, sums)])

    def owned(ch, li):
        own = ch[6][...] == li
        return jnp.concatenate([own] * (n_heads // 2), axis=1)

    accs = [jnp.where(owned(ch, n_lev), scores(ch[0], ch[1]), 0.0) for ch in chains]
    for li in range(n_lev):
        for i, (ch, ex) in enumerate(zip(chains, exs)):
            e = ex[2 + li].astype(BF16)
            accs[i] = jnp.where(owned(ch, li), scores(ch[0] * e, ch[1] * e), accs[i])
    outs = [_dot(acc.astype(BF16), head_stack(ch[2])) for ch, acc in zip(chains, accs)]
    news = [_dot(value_rows(ch[2]), head_stack(ch[1] * ex[1].astype(BF16)))
            for ch, ex in zip(chains, exs)]
    for i, (ch, ex) in enumerate(zip(chains, exs)):
        st_ref = ch[4]
        st = st_ref[...]
        outs[i] = outs[i] + _dot_nt(ch[0] * ex[0].astype(BF16), head_stack(st.astype(BF16)))
        st_ref[...] = st * e_tots[i] + news[i]
    return outs


def _head_rmsnorm_gate(o, gain, gate, bdn):
    ms = _dot((o * o).astype(BF16), bdn)
    return o * lax.rsqrt(ms + EPS) * gain * _silu(gate)


def _bidir_scan(prologue, prep_f, prep_b, gate_fn, gain_ref, consts, o_ref, of_ref, ob_ref, stf_ref,
                stb_ref, n_heads, min_vpu_block):
    w_f, w_b, lev_f, lev_b, bdn = consts
    s = of_ref.shape[0]
    c = SCAN_CHUNK
    group = min(SCAN_CHUNKS_PER_STEP, s // c)
    block = group * c
    assert s % block == 0
    nb = s // block
    stf_ref[...] = jnp.zeros_like(stf_ref)
    stb_ref[...] = jnp.zeros_like(stb_ref)

    def prologue_step(jb, carry):
        prologue(pl.multiple_of(jb * block, block), block)
        return carry

    lax.fori_loop(0, nb, prologue_step, 0, unroll=True)

    def step(j, carry):
        rows, chains = [], []
        for g in range(group):
            rf = pl.multiple_of(j * block + g * c, c)
            rb = pl.multiple_of((nb - 1 - j) * block + (group - 1 - g) * c, c)
            rows += [(of_ref, rf), (ob_ref, rb)]
            chains += [prep_f(rf) + (stf_ref, w_f, lev_f, False), prep_b(rb) + (stb_ref, w_b, lev_b, True)]
        outs = _scan_chunks(chains, n_heads, min_vpu_block)
        for (ref, r0), o in zip(rows, outs):
            ref[pl.ds(r0, c), :] = o
        return carry

    lax.fori_loop(0, nb, step, 0)
    for jb in range(nb):
        rows = pl.ds(jb * block, block)
        y = _head_rmsnorm_gate(of_ref[rows, :] + ob_ref[rows, :], gain_ref[...],
                               gate_fn(jb * block, block), bdn[...])
        o_ref[0, rows, :] = y.astype(o_ref.dtype)


def _hgrn_kernel(pb_ref, pf_ref, lb_ref, gain_ref, *refs):
    consts = refs[:len(SCAN_CONST_NAMES)]
    (o_ref, of_ref, ob_ref, stf_ref, stb_ref, q_ref, kf_ref, kb_ref, laf_ref,
     lab_ref) = refs[len(SCAN_CONST_NAMES):]
    k_refs = (kf_ref, kb_ref)
    la_refs = (laf_ref, lab_ref)
    c = SCAN_CHUNK
    wd = HB_WIDTH

    def prologue(r0, rows):
        q_ref[pl.ds(r0, rows), :] = (_silu(pb_ref[0, pl.ds(r0, rows), 0:wd].astype(F32))
                                     * HB_HEAD_DIM ** -0.5).astype(q_ref.dtype)
        for direction in range(2):
            lb = lb_ref[direction:direction + 1, :]
            z = pf_ref[0, pl.ds(r0, rows), direction * wd:(direction + 1) * wd]
            nz2 = z * -LOG2_E
            e = jnp.exp2(-jnp.abs(nz2))
            r = 1.0 / (1.0 + e)
            cap = jnp.exp2(jnp.minimum(nz2, MAX_NEG_LOGIT * LOG2_E))
            la_refs[direction][pl.ds(r0, rows), :] = (
                jnp.log((1.0 + lb * cap) * r) * LOG2_E - jnp.maximum(nz2, 0.0))
            k_refs[direction][pl.ds(r0, rows), :] = (
                (1.0 - lb) * jnp.where(z >= 0.0, e * r, r)).astype(k_refs[direction].dtype)

    def prep(direction):
        def fn(r0):
            rows = pl.ds(r0, c)
            return (q_ref[rows, :], k_refs[direction][rows, :], pb_ref[0, rows, wd:2 * wd],
                    la_refs[direction][rows, :])
        return fn

    gate_fn = lambda r0, rows: pb_ref[0, pl.ds(r0, rows), 2 * wd:3 * wd].astype(F32)
    _bidir_scan(prologue, prep(0), prep(1), gate_fn, gain_ref, consts, o_ref, of_ref, ob_ref, stf_ref,
                stb_ref, HB_HEADS, HGRN_MIN_VPU_BLOCK)


def _gla_kernel(pc_ref, plr_ref, w2h_ref, w2l_ref, gb_ref, gain_ref, *refs):
    consts = refs[:len(SCAN_CONST_NAMES)]
    o_ref, of_ref, ob_ref, stf_ref, stb_ref, la_ref = refs[len(SCAN_CONST_NAMES):]
    c = SCAN_CHUNK
    kw = GC_KEY_WIDTH
    vw = GC_WIDTH

    def prologue(r0, rows):
        lr_hi, lr_lo = _split_bf16(plr_ref[0, pl.ds(r0, rows), :])
        w2h = w2h_ref[...]
        z = _dot(lr_hi, w2h) + _dot(lr_lo, w2h) + _dot(lr_hi, w2l_ref[...]) + gb_ref[...]
        la_ref[pl.ds(r0, rows), :] = _log_sigmoid(z) * (LOG2_E / GC_GATE_NORMALIZER)

    def prep(direction):
        def fn(r0):
            rows = pl.ds(r0, c)
            qkv = pc_ref[0, rows, :]
            q = (qkv[:, 0:kw].astype(F32) * GC_KEY_DIM ** -0.5).astype(BF16)
            k = qkv[:, kw:2 * kw]
            v = qkv[:, 2 * kw:2 * kw + vw]
            return q, k, v, la_ref[rows, direction * kw:(direction + 1) * kw]
        return fn

    gate_fn = lambda r0, rows: pc_ref[0, pl.ds(r0, rows), 2 * kw + vw:2 * kw + 2 * vw].astype(F32)
    _bidir_scan(prologue, prep(0), prep(1), gate_fn, gain_ref, consts, o_ref, of_ref, ob_ref, stf_ref,
                stb_ref, GC_HEADS, GLA_MIN_VPU_BLOCK)


def _scan_call(kernel, name, seq_inputs, small_inputs, consts, b, s, hk, hv, dv, token_scratch):
    const_arrays = [consts[n] for n in SCAN_CONST_NAMES]
    in_specs = ([pl.BlockSpec((1, s, a.shape[2]), lambda i: (i, 0, 0)) for a in seq_inputs]
                + [_const_spec(a.shape) for a in small_inputs]
                + [_const_spec(a.shape) for a in const_arrays])
    return pl.pallas_call(
        kernel,
        grid=(b,),
        in_specs=in_specs,
        out_specs=pl.BlockSpec((1, s, hv), lambda i: (i, 0, 0)),
        out_shape=jax.ShapeDtypeStruct((b, s, hv), BF16),
        scratch_shapes=[pltpu.VMEM((s, hv), F32), pltpu.VMEM((s, hv), F32),
                        pltpu.VMEM((dv, hk), F32), pltpu.VMEM((dv, hk), F32)]
        + [pltpu.VMEM((s, w), dt) for w, dt in token_scratch],
        compiler_params=_params("parallel"),
        name=name,
    )(*seq_inputs, *small_inputs, *const_arrays)


FF_CHUNK = 1024


def _post_kernel(x_ref, oa_ref, ob_ref, oc_ref, pg_ref, gtm_ref, scf_ref, shf_ref, gtf_ref, g2_ref,
                 wua_ref, wub_ref, wuc_ref, wo_ref, w1_ref, w2_ref, fg_ref, out_ref, *, final):
    d = x_ref.shape[2]
    gate = lambda i: _sigmoid(pg_ref[0, :, i * d:(i + 1) * d].astype(F32))
    merged = (gate(0) * _dot(oa_ref[0], wua_ref[...])
              + gate(1) * _dot(ob_ref[0], wub_ref[...])
              + gate(2) * _dot(oc_ref[0], wuc_ref[...]))
    x1 = x_ref[0] + gtm_ref[0] * _dot(merged.astype(BF16), wo_ref[...])
    ms = jnp.mean(x1 * x1, axis=-1, keepdims=True)
    y = x1 * lax.rsqrt(ms + EPS) * g2_ref[...]
    h = (y * (1.0 + scf_ref[0]) + shf_ref[0]).astype(BF16)
    acc = jnp.zeros_like(x1)
    for c0 in range(0, w1_ref.shape[1], FF_CHUNK):
        u = jnp.maximum(_dot(h, w1_ref[:, c0:c0 + FF_CHUNK]), 0.0)
        acc = acc + _dot((u * u).astype(BF16), w2_ref[c0:c0 + FF_CHUNK, :])
    x2 = x1 + gtf_ref[0] * acc
    if final:
        x2 = x2 * lax.rsqrt(jnp.mean(x2 * x2, axis=-1, keepdims=True) + EPS) * fg_ref[...]
    out_ref[0] = x2


def _layer_spec(stacked, layer):
    n = stacked.ndim - 1
    return pl.BlockSpec((None,) + stacked.shape[1:], lambda *_: (layer,) + (0,) * n,
                        pipeline_mode=pl.Buffered(1))


def _post_mix_mlp(x, oa, ob, oc, pg, gtm, scf, shf, gtf, g2, weights, layer, fg, final, tm):
    b, s, d = x.shape
    tok = lambda a: pl.BlockSpec((1, tm, a.shape[2]), lambda i, j: (i, j, 0))
    vec = pl.BlockSpec((1, 1, d), lambda i, j: (i, 0, 0))
    return pl.pallas_call(
        functools.partial(_post_kernel, final=final),
        grid=(b, s // tm),
        in_specs=[tok(x), tok(oa), tok(ob), tok(oc), tok(pg), vec, vec, vec, vec, _const_spec((1, d))]
        + [_layer_spec(w, layer) for w in weights] + [_const_spec((1, d))],
        out_specs=pl.BlockSpec((1, tm, d), lambda i, j: (i, j, 0)),
        out_shape=jax.ShapeDtypeStruct((b, s, d), F32),
        compiler_params=_params("parallel", "arbitrary"),
        name="merge_out_mlp",
    )(x, oa, ob, oc, pg, gtm, scf, shf, gtf, g2.reshape(1, d), *weights, fg.reshape(1, d))


def _input_projection_groups(w):
    o_b = 2 * DA_QK_WIDTH + DA_WIDTH
    o_c = o_b + 5 * HB_WIDTH
    o_lr = o_c + 2 * GC_KEY_WIDTH + 2 * GC_WIDTH
    o_g = o_lr + 2 * GC_GATE_RANK
    wb = HB_WIDTH
    attn = w[:, :o_b]
    hgrn_qig = jnp.concatenate([w[:, o_b:o_b + wb], w[:, o_b + 3 * wb:o_b + 5 * wb]], axis=1)
    hgrn_f = w[:, o_b + wb:o_b + 3 * wb]
    gla = w[:, o_c:o_lr]
    lr = jnp.pad(w[:, o_lr:o_g], ((0, 0), (0, LANES - 2 * GC_GATE_RANK)))
    gates = w[:, o_g:]
    return [g.astype(BF16) for g in (attn, hgrn_qig, hgrn_f, gla, lr, gates)]


PROJ_OUT_DTYPES = (BF16, BF16, F32, BF16, F32, BF16)


def kernel(x, c, ada_w, ada_b, norm_mix_g, norm_mlp_g, w_in, diff_lambda, diff_subln_g, hgrn_lb_logits,
           hgrn_norm_g, gla_gate_w2, gla_gate_b, gla_norm_g, w_up_a, w_up_b, w_up_c, w_out, mlp_w1, mlp_w2,
           final_norm_g):
    b, s, d = x.shape
    tm = min(512, s)
    tq = min(2048, s)
    mod = _modulation(c, ada_w, ada_b).reshape(DEPTH, b, N_MOD, 1, d)
    lower_bounds = _lower_bounds(hgrn_lb_logits)
    hgrn_consts = _scan_constants(SCAN_CHUNK, HB_HEADS, HB_HEAD_DIM, HB_HEAD_DIM, HGRN_MIN_VPU_BLOCK)
    gla_consts = _scan_constants(SCAN_CHUNK, GC_HEADS, GC_KEY_DIM, GC_VAL_DIM, GLA_MIN_VPU_BLOCK)
    post_weights = [w.astype(BF16) for w in (w_up_a, w_up_b, w_up_c, w_out, mlp_w1, mlp_w2)]
    for l in range(DEPTH):
        sh_m, sc_m, gt_m, sh_f, sc_f, gt_f = [mod[l, :, i] for i in range(N_MOD)]
        pa, pb, pf, pc, plr, pg = _pre_project(
            x, sc_m, sh_m, norm_mix_g[l], _input_projection_groups(w_in[l]), PROJ_OUT_DTYPES, tm)
        o_a = _diff_attention(pa, diff_lambda[l], diff_subln_g[l], l, tq)
        o_b = _scan_call(
            _hgrn_kernel, "hgrn2_scan", [pb, pf],
            [lower_bounds[l], jnp.tile(hgrn_norm_g[l], HB_HEADS).reshape(1, HB_WIDTH)],
            hgrn_consts, b, s, HB_WIDTH, HB_WIDTH, HB_HEAD_DIM,
            [(HB_WIDTH, BF16)] * 3 + [(HB_WIDTH, F32)] * 2)
        w2 = jnp.zeros((LANES, 2 * GC_KEY_WIDTH), F32)
        w2 = w2.at[0:GC_GATE_RANK, 0:GC_KEY_WIDTH].set(gla_gate_w2[l, 0])
        w2 = w2.at[GC_GATE_RANK:2 * GC_GATE_RANK, GC_KEY_WIDTH:].set(gla_gate_w2[l, 1])
        w2_hi = w2.astype(BF16)
        w2_lo = (w2 - w2_hi.astype(F32)).astype(BF16)
        o_c = _scan_call(
            _gla_kernel, "gla_scan", [pc, plr],
            [w2_hi, w2_lo, gla_gate_b[l].reshape(1, 2 * GC_KEY_WIDTH),
             jnp.tile(gla_norm_g[l], GC_HEADS).reshape(1, GC_WIDTH)],
            gla_consts, b, s, GC_KEY_WIDTH, GC_WIDTH, GC_VAL_DIM, [(2 * GC_KEY_WIDTH, F32)])
        x = _post_mix_mlp(x, o_a, o_b, o_c, pg, gt_m, sc_f, sh_f, gt_f, norm_mlp_g[l], post_weights, l,
                          final_norm_g, l == DEPTH - 1, tm)
    return x
```
